```python
import math
import jax, jax.numpy as jnp
from jax import lax
import numpy as np

D_MODEL = 2048
BATCH = 16
SEQ = 256
DEPTH = 4
DEC_BATCH = 4
DEC_SEQ = 4096
PAST_LEN = 512

GRID_W = 64
CHUNK = 128
H_RET = 8
DK_RET = 128
DV_RET = 128
D_RET = H_RET * DV_RET
H_SSD = 16
P_SSD = 64
D_SSD = H_SSD * P_SSD
G_SSD = 2
N_SSD = 128
D_CONV = 3
D_XBC = D_SSD + 2 * G_SSD * N_SSD
D_FF = 4 * D_MODEL
D_IN = 3 * H_RET * DK_RET // 1 - 2 * H_RET * DK_RET + 2 * H_RET * DK_RET + D_RET + D_SSD + D_XBC + 2 * H_SSD
N_MOD = 6
EPS = 1e-6
ROPE_BASE = 10000.0
SPLITS = [H_RET * DK_RET, 2 * H_RET * DK_RET, 2 * H_RET * DK_RET + D_RET,
          2 * H_RET * DK_RET + 2 * D_RET, 2 * H_RET * DK_RET + 2 * D_RET + D_SSD,
          2 * H_RET * DK_RET + 2 * D_RET + D_SSD + D_XBC]

kernel_name = "hybrid_retention_ssd_diffusion_step"


def rms_norm(x, w):
    xf = x.astype(jnp.float32)
    y = xf * lax.rsqrt(jnp.mean(xf * xf, -1, keepdims=True) + EPS)
    return (y * w.astype(jnp.float32)).astype(x.dtype)


def head_rms(y):
    yf = y.astype(jnp.float32)
    return (yf * lax.rsqrt(jnp.mean(yf * yf, -1, keepdims=True) + EPS)).astype(y.dtype)


def grid_rope(length):
    rows = length // GRID_W
    pos = jnp.arange(rows * GRID_W)
    row = (pos // GRID_W).astype(jnp.float32)
    col = (pos % GRID_W).astype(jnp.float32)
    half = DK_RET // 2
    inv = 1.0 / (ROPE_BASE ** (jnp.arange(0, half, 2, dtype=jnp.float32) / half))
    ang = jnp.concatenate([row[:, None] * inv, col[:, None] * inv], -1)
    return jnp.cos(ang), jnp.sin(ang)


def apply_rope(x, cos, sin):
    c = cos[None, :, None, :].astype(x.dtype)
    s = sin[None, :, None, :].astype(x.dtype)
    x1, x2 = x[..., :DK_RET // 2], x[..., DK_RET // 2:]
    return jnp.concatenate([x1 * c - x2 * s, x1 * s + x2 * c], -1)


def chunked_scan(cq, kb, xv, log_a, init):
    b, l, h, n = kb.shape
    p = xv.shape[-1]
    nc = l // CHUNK
    cq = cq.reshape(b, nc, CHUNK, h, n)
    kb = kb.reshape(b, nc, CHUNK, h, n)
    xv = xv.reshape(b, nc, CHUNK, h, p)
    a = jnp.moveaxis(log_a.astype(jnp.float32).reshape(b, nc, CHUNK, h), -1, 1)
    a_cum = jnp.cumsum(a, -1)
    lower = jnp.tril(jnp.ones((CHUNK, CHUNK), bool))
    seg = a_cum[..., :, None] - a_cum[..., None, :]
    decay_in = jnp.where(lower, jnp.exp(jnp.where(lower, seg, 0.0)), 0.0)
    y_in = jnp.einsum("bclhn,bcshn,bhcls,bcshp->bclhp", cq, kb, decay_in.astype(xv.dtype), xv)
    decay_st = jnp.exp(a_cum[..., -1:] - a_cum)
    chunk_st = jnp.einsum("bclhn,bhcl,bclhp->bchpn", kb, decay_st.astype(xv.dtype), xv)
    chunk_decay = jnp.exp(a_cum[..., -1])

    def step(s, inp):
        st_c, dec_c = inp
        return dec_c[..., None, None] * s + st_c, s

    final, starts = lax.scan(step, init.astype(jnp.float32),
                             (jnp.moveaxis(chunk_st, 1, 0).astype(jnp.float32),
                              jnp.moveaxis(chunk_decay, 2, 0)))
    starts = jnp.moveaxis(starts, 0, 1)
    y_off = jnp.einsum("bclhn,bchpn,bhcl->bclhp", cq, starts.astype(xv.dtype),
                       jnp.exp(a_cum).astype(xv.dtype))
    y = (y_in + y_off).reshape(b, l, h, p)
    return y, final.astype(xv.dtype)


def bidir_scan(cq, kb, xv_f, xv_b, log_a_f, log_a_b, init_f, init_b):
    y_f, s_f = chunked_scan(cq, kb, xv_f, log_a_f, init_f)
    flip = lambda t: jnp.flip(t, 1)
    y_b, s_b = chunked_scan(flip(cq), flip(kb), flip(xv_b), flip(log_a_b), init_b)
    return y_f + flip(y_b), s_f, s_b


def dw_conv(x, w, bias):
    y = lax.conv_general_dilated(x, w[:, None, :].astype(x.dtype), window_strides=(1,),
                                 padding=[(D_CONV // 2, D_CONV // 2)],
                                 dimension_numbers=("NWC", "WIO", "NWC"),
                                 feature_group_count=x.shape[-1])
    return y + bias


def mixer(h, rope, init_ret, init_ssd, w_in, ret_log_decay, conv_w, conv_b, dt_bias, a_log,
          d_skip, ssd_norm_w, w_out):
    b, l, _ = h.shape
    proj = h @ w_in
    q, k, v, g, z, xbc, dt = jnp.split(proj, SPLITS, axis=-1)
    q = q.reshape(b, l, H_RET, DK_RET)
    k = k.reshape(b, l, H_RET, DK_RET)
    v = v.reshape(b, l, H_RET, DV_RET)
    if rope is not None:
        q = apply_rope(q, *rope)
        k = apply_rope(k, *rope)
    la = lambda d: jnp.broadcast_to(ret_log_decay[d].astype(jnp.float32), (b, l, H_RET))
    y_r, r_f, r_b = bidir_scan(q * (DK_RET ** -0.5), k, v, v, la(0), la(1), init_ret[0], init_ret[1])
    y_r = jax.nn.silu(g) * head_rms(y_r).reshape(b, l, D_RET)
    xbc = jax.nn.silu(dw_conv(xbc, conv_w, conv_b))
    xs, bm, cm = jnp.split(xbc, [D_SSD, D_SSD + G_SSD * N_SSD], axis=-1)
    xs = xs.reshape(b, l, H_SSD, P_SSD)
    rep = H_SSD // G_SSD
    bm = jnp.repeat(bm.reshape(b, l, G_SSD, N_SSD), rep, axis=2)
    cm = jnp.repeat(cm.reshape(b, l, G_SSD, N_SSD), rep, axis=2)
    dt = jax.nn.softplus(dt.astype(jnp.float32).reshape(b, l, 2, H_SSD) + dt_bias.astype(jnp.float32))
    a = -jnp.exp(a_log.astype(jnp.float32))
    xdt = lambda d: xs * dt[:, :, d, :, None].astype(xs.dtype)
    y_s, s_f, s_b = bidir_scan(cm, bm, xdt(0), xdt(1), dt[:, :, 0] * a[0], dt[:, :, 1] * a[1],
                               init_ssd[0], init_ssd[1])
    y_s = y_s + d_skip[:, None] * xs
    y_s = rms_norm(y_s.reshape(b, l, D_SSD) * jax.nn.silu(z), ssd_norm_w)
    out = jnp.concatenate([y_r, y_s], -1) @ w_out
    return out, (r_f, r_b, s_f, s_b)


def setup_inputs(seed: int = 0) -> dict:
    key = jax.random.key(seed)
    ks = jax.random.split(key, 24)
    f32 = jnp.float32
    nrm = lambda k, s, sc: jax.random.normal(k, s, f32) * sc
    base_decay = jnp.log(1.0 - 2.0 ** (-5.0 - jnp.arange(H_RET, dtype=f32)))
    ret_log_decay = base_decay[None, None, :] * jnp.exp(nrm(ks[10], (DEPTH, 2, H_RET), 0.05))
    dt0 = jnp.exp(jax.random.uniform(ks[11], (DEPTH, 2, H_SSD), f32, math.log(1e-3), math.log(1e-1)))
    dt_bias = dt0 + jnp.log(-jnp.expm1(-dt0))
    a_log = jnp.log(jax.random.uniform(ks[12], (DEPTH, 2, H_SSD), f32, 1.0, 16.0))
    return {
        "x_prompt": nrm(ks[0], (BATCH, SEQ, D_MODEL), 1.0),
        "x_sample": nrm(ks[1], (DEC_BATCH, DEC_SEQ, D_MODEL), 1.0),
        "state_ret": nrm(ks[2], (DEC_BATCH, DEPTH, 2, H_RET, DV_RET, DK_RET), 1.0),
        "state_ssd": nrm(ks[3], (DEC_BATCH, DEPTH, 2, H_SSD, P_SSD, N_SSD), 1.0),
        "c": nrm(ks[4], (DEC_BATCH, D_MODEL), 1.0),
        "c_ctx": nrm(ks[5], (D_MODEL,), 1.0),
        "w_ada": nrm(ks[6], (DEPTH, D_MODEL, N_MOD * D_MODEL), 0.5 * D_MODEL ** -0.5),
        "b_ada": nrm(ks[7], (DEPTH, N_MOD * D_MODEL), 0.02),
        "norm1_w": 1.0 + nrm(ks[8], (DEPTH, D_MODEL), 0.02),
        "w_in": nrm(ks[9], (DEPTH, D_MODEL, D_IN), D_MODEL ** -0.5),
        "ret_log_decay": ret_log_decay,
        "conv_w": nrm(ks[13], (DEPTH, D_CONV, D_XBC), D_CONV ** -0.5),
        "conv_b": nrm(ks[14], (DEPTH, D_XBC), 0.02),
        "dt_bias": dt_bias,
        "a_log": a_log,
        "d_skip": 1.0 + nrm(ks[15], (DEPTH, H_SSD), 0.02),
        "ssd_norm_w": 1.0 + nrm(ks[16], (DEPTH, D_SSD), 0.02),
        "w_out": nrm(ks[17], (DEPTH, D_RET + D_SSD, D_MODEL), (D_RET + D_SSD) ** -0.5),
        "norm2_w": 1.0 + nrm(ks[18], (DEPTH, D_MODEL), 0.02),
        "w_ff1": nrm(ks[19], (DEPTH, D_MODEL, D_FF), D_MODEL ** -0.5),
        "w_ff2": nrm(ks[20], (DEPTH, D_FF, D_MODEL), D_FF ** -0.5),
        "final_norm_w": 1.0 + nrm(ks[21], (D_MODEL,), 0.02),
    }


def reference(x_prompt, x_sample, state_ret, state_ssd, c, c_ctx, w_ada, b_ada, norm1_w, w_in,
              ret_log_decay, conv_w, conv_b, dt_bias, a_log, d_skip, ssd_norm_w, w_out, norm2_w,
              w_ff1, w_ff2, final_norm_w):
    def run_layer(x, cond, l, rope, init_ret, init_ssd):
        mod = (jax.nn.silu(cond) @ w_ada[l] + b_ada[l])[:, None, :]
        sh1, sc1, g1, sh2, sc2, g2 = jnp.split(mod, N_MOD, axis=-1)
        h = rms_norm(x, norm1_w[l]) * (1 + sc1) + sh1
        m, st = mixer(h, rope, init_ret, init_ssd, w_in[l], ret_log_decay[l], conv_w[l], conv_b[l],
                      dt_bias[l], a_log[l], d_skip[l], ssd_norm_w[l], w_out[l])
        x = x + g1 * m
        h = rms_norm(x, norm2_w[l]) * (1 + sc2) + sh2
        x = x + g2 * (jnp.square(jax.nn.relu(h @ w_ff1[l])) @ w_ff2[l])
        return x, st

    bp = x_prompt.shape[0]
    zr = jnp.zeros((bp, H_RET, DV_RET, DK_RET), x_prompt.dtype)
    zs = jnp.zeros((bp, H_SSD, P_SSD, N_SSD), x_prompt.dtype)
    xp = x_prompt
    ret_states = []
    ssd_states = []
    for l in range(DEPTH):
        xp, (rf, rb, sf, sb) = run_layer(xp, c_ctx[None, :], l, None, (zr, zr), (zs, zs))
        ret_states.append(jnp.stack([rf, rb], 1))
        ssd_states.append(jnp.stack([sf, sb], 1))
    new_state_ret = jnp.stack(ret_states, 1)
    new_state_ssd = jnp.stack(ssd_states, 1)
    y_prompt = rms_norm(xp, final_norm_w)

    rope = grid_rope(x_sample.shape[1])
    xs = x_sample
    for l in range(DEPTH):
        xs, _ = run_layer(xs, c, l, rope,
                          (state_ret[:, l, 0], state_ret[:, l, 1]),
                          (state_ssd[:, l, 0], state_ssd[:, l, 1]))
    y_sample = rms_norm(xs, final_norm_w)
    return (y_prompt, y_sample, new_state_ret, new_state_ssd)
```

```python
import functools

import jax
import jax.numpy as jnp
from jax import lax
from jax.experimental import pallas as pl
from jax.experimental.pallas import tpu as pltpu

F32 = jnp.float32
BF16 = jnp.bfloat16

D_MODEL = 2048
DEPTH = 4
CHUNK = 128
H_RET = 8
DK_RET = 128
DV_RET = 128
D_RET = H_RET * DV_RET
H_SSD = 16
P_SSD = 64
D_SSD = H_SSD * P_SSD
G_SSD = 2
N_SSD = 128
D_CONV = 3
D_XBC = D_SSD + 2 * G_SSD * N_SSD
D_FF = 4 * D_MODEL
N_MOD = 6
EPS = 1e-6
ROPE_BASE = 10000.0
GRID_W = 64

D_MAIN = 4 * D_RET + D_SSD + D_XBC
OFF_Q, OFF_K, OFF_V, OFF_G = 0, D_RET, 2 * D_RET, 3 * D_RET
OFF_Z = 4 * D_RET
OFF_XS = OFF_Z + D_SSD
OFF_B = OFF_XS + D_SSD
OFF_C = OFF_B + G_SSD * N_SSD

HQ = 4
NQ = H_SSD // HQ
DT_LANES = 128
COND_ROWS = 16
V7X_VMEM_LIMIT = 56 * 1024 * 1024

NT_DIMS = (((1,), (1,)), ((), ()))


def _silu(x):
    return x * jax.nn.sigmoid(x)


def _softplus(x):
    return jnp.maximum(x, 0.0) + jnp.log1p(jnp.exp(-jnp.abs(x)))


def _rms(x, w):
    return x * lax.rsqrt(jnp.mean(x * x, axis=-1, keepdims=True) + EPS) * w


def _params(sem):
    return pltpu.CompilerParams(dimension_semantics=sem, vmem_limit_bytes=V7X_VMEM_LIMIT)


def _ada_kernel(cond_ref, w_ref, b_ref, o_ref):
    s = _silu(cond_ref[...]).astype(BF16)
    o_ref[...] = jnp.dot(s, w_ref[...].astype(BF16), preferred_element_type=F32) + b_ref[...]


def _ada(cond, w_ada, b_ada):
    tn = 1024
    n = N_MOD * D_MODEL
    return pl.pallas_call(
        _ada_kernel,
        grid=(DEPTH, n // tn),
        in_specs=[
            pl.BlockSpec((COND_ROWS, D_MODEL), lambda l, j: (0, 0)),
            pl.BlockSpec((None, D_MODEL, tn), lambda l, j: (l, 0, j)),
            pl.BlockSpec((None, 1, tn), lambda l, j: (l, 0, j)),
        ],
        out_specs=pl.BlockSpec((None, COND_ROWS, tn), lambda l, j: (l, 0, j)),
        out_shape=jax.ShapeDtypeStruct((DEPTH, COND_ROWS, n), F32),
        compiler_params=_params(("arbitrary", "arbitrary")),
        name="ada_mod",
    )(cond, w_ada, b_ada.reshape(DEPTH, 1, n))


def _in_kernel(x_ref, nw_ref, sc_ref, sh_ref, w_ref, wdt_ref, o_ref, dt_ref, h_scr):
    @pl.when(pl.program_id(1) == 0)
    def _():
        h = _rms(x_ref[...], nw_ref[...]) * (1.0 + sc_ref[...]) + sh_ref[...]
        hb = h.astype(BF16)
        h_scr[...] = hb
        dt_ref[...] = jnp.dot(hb, wdt_ref[...], preferred_element_type=F32)

    o_ref[...] = jnp.dot(h_scr[...], w_ref[...], preferred_element_type=F32).astype(o_ref.dtype)


def _in_proj(x2, rows_per_cond, nw, sc, sh, w_main, w_dt):
    m = x2.shape[0]
    tm, tn = 1024, 512
    cond_map = lambda i, j: ((i * tm) // rows_per_cond, 0, 0)
    return pl.pallas_call(
        _in_kernel,
        grid=(m // tm, D_MAIN // tn),
        in_specs=[
            pl.BlockSpec((tm, D_MODEL), lambda i, j: (i, 0)),
            pl.BlockSpec((1, D_MODEL), lambda i, j: (0, 0)),
            pl.BlockSpec((None, 1, D_MODEL), cond_map),
            pl.BlockSpec((None, 1, D_MODEL), cond_map),
            pl.BlockSpec((D_MODEL, tn), lambda i, j: (0, j)),
            pl.BlockSpec((D_MODEL, NQ * DT_LANES), lambda i, j: (0, 0)),
        ],
        out_specs=[
            pl.BlockSpec((tm, tn), lambda i, j: (i, j)),
            pl.BlockSpec((tm, NQ * DT_LANES), lambda i, j: (i, 0)),
        ],
        out_shape=[
            jax.ShapeDtypeStruct((m, D_MAIN), BF16),
            jax.ShapeDtypeStruct((m, NQ * DT_LANES), F32),
        ],
        scratch_shapes=[pltpu.VMEM((tm, D_MODEL), BF16)],
        compiler_params=_params(("arbitrary", "arbitrary")),
        name="in_proj",
    )(x2, nw, sc, sh, w_main, w_dt)


def _ret_kernel(*refs, seq, rope, has_init, want_state):
    it = iter(refs)
    ld_ref, q_ref, k_ref, v_ref, g_ref = (next(it) for _ in range(5))
    cos_ref = sin_ref = init_ref = st_ref = None
    if rope:
        cos_ref, sin_ref = next(it), next(it)
    if has_init:
        init_ref = next(it)
    y_ref = next(it)
    if want_state:
        st_ref = next(it)
    q_scr, k_scr, sf_scr = next(it), next(it), next(it)

    t = CHUNK
    nc = seq // t
    head = pl.program_id(1)
    la_f = ld_ref[0, head]
    la_b = ld_ref[1, head]

    ii = lax.broadcasted_iota(jnp.int32, (t, t), 0)
    jj = lax.broadcasted_iota(jnp.int32, (t, t), 1)
    dist = (ii - jj).astype(F32)
    lower = ii >= jj
    upper = ii <= jj
    dmat = (jnp.where(lower, jnp.exp(la_f * jnp.where(lower, dist, 0.0)), 0.0)
            + jnp.where(upper, jnp.exp(la_b * jnp.where(upper, -dist, 0.0)), 0.0))
    pos = lax.broadcasted_iota(jnp.int32, (t, 1), 0).astype(F32)
    e_f = jnp.exp(la_f * (pos + 1.0))
    e_b = jnp.exp(la_b * (t - pos))
    w_f = jnp.exp(la_f * (t - 1.0 - pos))
    w_b = jnp.exp(la_b * pos)
    dec_f = jnp.exp(jnp.full((1, 1), t, F32) * la_f)
    dec_b = jnp.exp(jnp.full((1, 1), t, F32) * la_b)
    scale = DK_RET ** -0.5

    def rows(c):
        return pl.ds(pl.multiple_of(c * t, t), t)

    def prep(c, carry):
        r = rows(c)
        q = q_ref[r, :].astype(F32)
        k = k_ref[r, :].astype(F32)
        if rope:
            cs, sn = cos_ref[r, :], sin_ref[r, :]
            q = q * cs + pltpu.roll(q, DK_RET // 2, 1) * sn
            k = k * cs + pltpu.roll(k, DK_RET // 2, 1) * sn
        q_scr[r, :] = (q * scale).astype(BF16)
        k_scr[r, :] = k.astype(BF16)
        return carry

    lax.fori_loop(0, nc, prep, 0)

    def chunk_state(r, w):
        kw = (k_scr[r, :].astype(F32) * w).T.astype(BF16)
        return jnp.dot(kw, v_ref[r, :], preferred_element_type=F32)

    def fwd(c, s):
        r = rows(c)
        sf_scr[c] = s.astype(BF16)
        return dec_f * s + chunk_state(r, w_f)

    def bwd(i, s):
        c = nc - 1 - i
        r = rows(c)
        q = q_scr[r, :]
        v = v_ref[r, :]
        att = lax.dot_general(q, k_scr[r, :], NT_DIMS, preferred_element_type=F32)
        p = (att * dmat).astype(BF16)
        qf = q.astype(F32)
        y = jnp.dot(p, v, preferred_element_type=F32)
        y = y + jnp.dot((qf * e_f).astype(BF16), sf_scr[c], preferred_element_type=F32)
        y = y + jnp.dot((qf * e_b).astype(BF16), s.astype(BF16), preferred_element_type=F32)
        yn = y * lax.rsqrt(jnp.mean(y * y, axis=-1, keepdims=True) + EPS)
        y_ref[r, :] = (_silu(g_ref[r, :].astype(F32)) * yn).astype(y_ref.dtype)
        return dec_b * s + chunk_state(r, w_b)

    if has_init:
        s0_f, s0_b = init_ref[0].T, init_ref[1].T
    else:
        s0_f = s0_b = jnp.zeros((DK_RET, DV_RET), F32)
    s_f = lax.fori_loop(0, nc, fwd, s0_f)
    s_b = lax.fori_loop(0, nc, bwd, s0_b)
    if want_state:
        st_ref[0] = s_f.T
        st_ref[1] = s_b.T


def _retention(proj3, ret_ld, rope_tabs, init, want_state):
    b, seq, _ = proj3.shape
    hblk = lambda off: pl.BlockSpec((None, seq, DK_RET), lambda i, h, o=off // DK_RET: (i, 0, o + h))
    in_specs = [pl.BlockSpec(memory_space=pltpu.SMEM), hblk(OFF_Q), hblk(OFF_K), hblk(OFF_V), hblk(OFF_G)]
    args = [ret_ld, proj3, proj3, proj3, proj3]
    if rope_tabs is not None:
        in_specs += [pl.BlockSpec((seq, DK_RET), lambda i, h: (0, 0))] * 2
        args += list(rope_tabs)
    st_spec = pl.BlockSpec((None, 2, None, DV_RET, DK_RET), lambda i, h: (i, 0, h, 0, 0))
    if init is not None:
        in_specs.append(st_spec)
        args.append(init)
    out_specs = [pl.BlockSpec((None, seq, DV_RET), lambda i, h: (i, 0, h))]
    out_shape = [jax.ShapeDtypeStruct((b, seq, D_RET), BF16)]
    if want_state:
        out_specs.append(st_spec)
        out_shape.append(jax.ShapeDtypeStruct((b, 2, H_RET, DV_RET, DK_RET), F32))
    return pl.pallas_call(
        functools.partial(_ret_kernel, seq=seq, rope=rope_tabs is not None,
                          has_init=init is not None, want_state=want_state),
        grid=(b, H_RET),
        in_specs=in_specs,
        out_specs=out_specs,
        out_shape=out_shape,
        scratch_shapes=[pltpu.VMEM((seq, DK_RET), BF16), pltpu.VMEM((seq, DK_RET), BF16),
                        pltpu.VMEM((seq // CHUNK, DK_RET, DV_RET), BF16)],
        compiler_params=_params(("arbitrary", "arbitrary")),
        name="retention",
    )(*args)


def _conv_silu(src_ref, w_ref, b_ref, dst_ref, seq):
    blk = min(seq, 256)
    nb = seq // blk
    w0, w1, w2, bias = w_ref[0:1, :], w_ref[1:2, :], w_ref[2:3, :], b_ref[...]
    row = lax.broadcasted_iota(jnp.int32, (blk, 1), 0)

    def body(i, carry):
        r0 = pl.multiple_of(i * blk, blk)
        cur = src_ref[pl.ds(r0, blk), :].astype(F32)
        p0 = pl.multiple_of(jnp.maximum(r0 - 16, 0), 16)
        n0 = pl.multiple_of(jnp.minimum(r0 + blk, seq - 16), 16)
        prev = jnp.where(i > 0, src_ref[pl.ds(p0, 16), :].astype(F32)[15:16, :], 0.0)
        nxt = jnp.where(i < nb - 1, src_ref[pl.ds(n0, 16), :].astype(F32)[0:1, :], 0.0)
        xm1 = jnp.where(row == 0, prev, pltpu.roll(cur, 1, 0))
        xp1 = jnp.where(row == blk - 1, nxt, pltpu.roll(cur, blk - 1, 0))
        y = xm1 * w0 + cur * w1 + xp1 * w2 + bias
        dst_ref[pl.ds(r0, blk), :] = _silu(y).astype(dst_ref.dtype)
        return carry

    lax.fori_loop(0, nb, body, 0)


def _ssd_kernel(*refs, seq, has_init, want_state):
    it = iter(refs)
    (dsk_ref, xs_ref, z_ref, b_ref, c_ref, dt_ref, bias_ref, alog_ref,
     wx_ref, bx_ref, wb_ref, bb_ref, wc_ref, bc_ref) = (next(it) for _ in range(14))
    init_ref = st_ref = None
    if has_init:
        init_ref = next(it)
    u_ref = next(it)
    if want_state:
        st_ref = next(it)
    xc_scr, bc_scr, cc_scr, dt_scr, da_scr, sf_scr = (next(it) for _ in range(6))

    t = CHUNK
    nc = seq // t
    quad = pl.program_id(1)

    _conv_silu(xs_ref, wx_ref, bx_ref, xc_scr, seq)
    _conv_silu(b_ref, wb_ref, bb_ref, bc_scr, seq)
    _conv_silu(c_ref, wc_ref, bc_ref, cc_scr, seq)

    a_neg = -jnp.exp(alog_ref[...])
    dt_bias = bias_ref[...]

    def prep(c, carry):
        r = pl.ds(pl.multiple_of(c * t, t), t)
        dt = _softplus(dt_ref[r, :] + dt_bias)
        dt_scr[r, :] = dt
        da_scr[r, :] = dt * a_neg
        return carry

    lax.fori_loop(0, nc, prep, 0)

    ii = lax.broadcasted_iota(jnp.int32, (t, t), 0)
    jj = lax.broadcasted_iota(jnp.int32, (t, t), 1)
    lower = ii >= jj
    upper = ii <= jj
    tril = lower.astype(F32)
    triu = upper.astype(F32)

    def rows(c):
        return pl.ds(pl.multiple_of(c * t, t), t)

    def cums(r):
        da = da_scr[r, :]
        cum = jnp.dot(tril, da, preferred_element_type=F32, precision=lax.Precision.HIGHEST)
        rcum = jnp.dot(triu, da, preferred_element_type=F32, precision=lax.Precision.HIGHEST)
        return cum, rcum

    def head_x(r, hh):
        return xc_scr[r, hh * P_SSD:(hh + 1) * P_SSD]

    def state_update(bf, w_col, x_h):
        bw = (bf * w_col).T.astype(BF16)
        return jnp.dot(bw, x_h, preferred_element_type=F32)

    def fwd(c, states):
        r = rows(c)
        cum, _ = cums(r)
        dt = dt_scr[r, :]
        bf = bc_scr[r, :].astype(F32)
        total = cum[t - 1:t, :]
        wst = jnp.exp(total - cum) * dt
        dec = jnp.exp(total)
        new = []
        for hh in range(HQ):
            s = states[hh]
            sf_scr[c * HQ + hh] = s.astype(BF16)
            new.append(dec[:, hh:hh + 1] * s + state_update(bf, wst[:, hh:hh + 1], head_x(r, hh)))
        return tuple(new)

    def bwd(i, states):
        c = nc - 1 - i
        r = rows(c)
        cum, rcum = cums(r)
        dt = dt_scr[r, :]
        cum_t, rcum_t, dt_t = cum.T, rcum.T, dt.T
        bmat = bc_scr[r, :]
        cmat = cc_scr[r, :]
        bf = bmat.astype(F32)
        cf = cmat.astype(F32)
        cb = lax.dot_general(cmat, bmat, NT_DIMS, preferred_element_type=F32)
        total_b = rcum[0:1, :]
        wst_b = jnp.exp(total_b - rcum) * dt
        dec_b = jnp.exp(total_b)
        e_f = jnp.exp(cum)
        e_b = jnp.exp(rcum)
        new = []
        outs = []
        for hh in range(HQ):
            fl, bl = hh, HQ + hh
            seg_f = cum[:, fl:fl + 1] - cum_t[fl:fl + 1, :]
            seg_b = rcum[:, bl:bl + 1] - rcum_t[bl:bl + 1, :]
            l_f = jnp.where(lower, jnp.exp(jnp.where(lower, seg_f, 0.0)), 0.0) * dt_t[fl:fl + 1, :]
            l_b = jnp.where(upper, jnp.exp(jnp.where(upper, seg_b, 0.0)), 0.0) * dt_t[bl:bl + 1, :]
            mh = (cb * (l_f + l_b)).astype(BF16)
            x_h = head_x(r, hh)
            s = states[hh]
            y = jnp.dot(mh, x_h, preferred_element_type=F32)
            y = y + jnp.dot((cf * e_f[:, fl:fl + 1]).astype(BF16), sf_scr[c * HQ + hh],
                            preferred_element_type=F32)
            y = y + jnp.dot((cf * e_b[:, bl:bl + 1]).astype(BF16), s.astype(BF16),
                            preferred_element_type=F32)
            y = y + dsk_ref[quad * HQ + hh] * x_h.astype(F32)
            zz = z_ref[r, hh * P_SSD:(hh + 1) * P_SSD].astype(F32)
            outs.append(y * _silu(zz))
            new.append(dec_b[:, bl:bl + 1] * s + state_update(bf, wst_b[:, bl:bl + 1], x_h))
        u_ref[r, :] = jnp.concatenate(outs, axis=1).astype(u_ref.dtype)
        return tuple(new)

    if has_init:
        s0_f = tuple(init_ref[0, hh].T for hh in range(HQ))
        s0_b = tuple(init_ref[1, hh].T for hh in range(HQ))
    else:
        s0_f = s0_b = tuple(jnp.zeros((N_SSD, P_SSD), F32) for _ in range(HQ))
    s_f = lax.fori_loop(0, nc, fwd, s0_f)
    s_b = lax.fori_loop(0, nc, bwd, s0_b)
    if want_state:
        for hh in range(HQ):
            st_ref[0, hh] = s_f[hh].T
            st_ref[1, hh] = s_b[hh].T


def _ssd(proj3, dt3, d_skip, dt_bias_q, a_log_q, conv_w, conv_b, init, want_state):
    b, seq, _ = proj3.shape
    wq = HQ * P_SSD
    col = lambda width, off, fn: pl.BlockSpec((None, seq, width),
                                              lambda i, q, o=off // width: (i, 0, o + fn(q)))
    ident = lambda q: q
    group = lambda q: q // (NQ // G_SSD)
    xoff = 0
    boff = D_SSD
    coff = D_SSD + G_SSD * N_SSD
    cw = lambda width, off, fn: pl.BlockSpec((D_CONV, width), lambda i, q, o=off // width: (0, o + fn(q)))
    cbias = lambda width, off, fn: pl.BlockSpec((1, width), lambda i, q, o=off // width: (0, o + fn(q)))
    in_specs = [
        pl.BlockSpec(memory_space=pltpu.SMEM),
        col(wq, OFF_XS, ident), col(wq, OFF_Z, ident), col(N_SSD, OFF_B, group), col(N_SSD, OFF_C, group),
        pl.BlockSpec((None, seq, DT_LANES), lambda i, q: (i, 0, q)),
        pl.BlockSpec((None, 1, DT_LANES), lambda i, q: (q, 0, 0)),
        pl.BlockSpec((None, 1, DT_LANES), lambda i, q: (q, 0, 0)),
        cw(wq, xoff, ident), cbias(wq, xoff, ident),
        cw(N_SSD, boff, group), cbias(N_SSD, boff, group),
        cw(N_SSD, coff, group), cbias(N_SSD, coff, group),
    ]
    args = [d_skip, proj3, proj3, proj3, proj3, dt3, dt_bias_q, a_log_q,
            conv_w, conv_b, conv_w, conv_b, conv_w, conv_b]
    st_spec = pl.BlockSpec((None, 2, HQ, P_SSD, N_SSD), lambda i, q: (i, 0, q, 0, 0))
    if init is not None:
        in_specs.append(st_spec)
        args.append(init)
    out_specs = [pl.BlockSpec((None, seq, wq), lambda i, q: (i, 0, q))]
    out_shape = [jax.ShapeDtypeStruct((b, seq, D_SSD), BF16)]
    if want_state:
        out_specs.append(st_spec)
        out_shape.append(jax.ShapeDtypeStruct((b, 2, H_SSD, P_SSD, N_SSD), F32))
    nc = seq // CHUNK
    return pl.pallas_call(
        functools.partial(_ssd_kernel, seq=seq, has_init=init is not None, want_state=want_state),
        grid=(b, NQ),
        in_specs=in_specs,
        out_specs=out_specs,
        out_shape=out_shape,
        scratch_shapes=[
            pltpu.VMEM((seq, wq), BF16), pltpu.VMEM((seq, N_SSD), BF16), pltpu.VMEM((seq, N_SSD), BF16),
            pltpu.VMEM((seq, DT_LANES), F32), pltpu.VMEM((seq, DT_LANES), F32),
            pltpu.VMEM((nc * HQ, N_SSD, P_SSD), BF16),
        ],
        compiler_params=_params(("arbitrary", "arbitrary")),
        name="ssd",
    )(*args)


def _out_kernel(yr_ref, us_ref, nw_ref, x_ref, g_ref, w_ref, o_ref, a_scr):
    @pl.when(pl.program_id(1) == 0)
    def _():
        a_scr[:, :D_RET] = yr_ref[...]
        a_scr[:, D_RET:] = _rms(us_ref[...].astype(F32), nw_ref[...]).astype(BF16)

    acc = jnp.dot(a_scr[...], w_ref[...], preferred_element_type=F32)
    o_ref[...] = x_ref[...] + g_ref[...] * acc


def _out_proj(yr, us, ssd_nw, x2, rows_per_cond, gate, w_out):
    m = x2.shape[0]
    tm, tn = 1024, 1024
    return pl.pallas_call(
        _out_kernel,
        grid=(m // tm, D_MODEL // tn),
        in_specs=[
            pl.BlockSpec((tm, D_RET), lambda i, j: (i, 0)),
            pl.BlockSpec((tm, D_SSD), lambda i, j: (i, 0)),
            pl.BlockSpec((1, D_SSD), lambda i, j: (0, 0)),
            pl.BlockSpec((tm, tn), lambda i, j: (i, j)),
            pl.BlockSpec((None, 1, tn), lambda i, j: ((i * tm) // rows_per_cond, 0, j)),
            pl.BlockSpec((D_RET + D_SSD, tn), lambda i, j: (0, j)),
        ],
        out_specs=pl.BlockSpec((tm, tn), lambda i, j: (i, j)),
        out_shape=jax.ShapeDtypeStruct((m, D_MODEL), F32),
        scratch_shapes=[pltpu.VMEM((tm, D_RET + D_SSD), BF16)],
        compiler_params=_params(("arbitrary", "arbitrary")),
        name="out_proj",
    )(yr, us, ssd_nw, x2, gate, w_out)


def _ffn_kernel(*refs, final):
    it = iter(refs)
    x_ref, nw_ref, sc_ref, sh_ref, g_ref, w1_ref, w2_ref = (next(it) for _ in range(7))
    fw_ref = next(it) if final else None
    o_ref, h_scr, acc_scr = next(it), next(it), next(it)
    f = pl.program_id(1)

    @pl.when(f == 0)
    def _():
        h = _rms(x_ref[...], nw_ref[...]) * (1.0 + sc_ref[...]) + sh_ref[...]
        h_scr[...] = h.astype(BF16)

    a = jnp.maximum(jnp.dot(h_scr[...], w1_ref[...], preferred_element_type=F32), 0.0)
    contrib = jnp.dot((a * a).astype(BF16), w2_ref[...], preferred_element_type=F32)

    @pl.when(f == 0)
    def _():
        acc_scr[...] = contrib

    @pl.when(f > 0)
    def _():
        acc_scr[...] += contrib

    @pl.when(f == pl.num_programs(1) - 1)
    def _():
        y = x_ref[...] + g_ref[...] * acc_scr[...]
        if final:
            y = _rms(y, fw_ref[...])
        o_ref[...] = y


def _ffn(x2, rows_per_cond, nw, sc, sh, gate, w1, w2, final_w):
    m = x2.shape[0]
    tm, tf = 512, 1024
    cond_map = lambda i, f: ((i * tm) // rows_per_cond, 0, 0)
    vec = pl.BlockSpec((1, D_MODEL), lambda i, f: (0, 0))
    cvec = pl.BlockSpec((None, 1, D_MODEL), cond_map)
    in_specs = [pl.BlockSpec((tm, D_MODEL), lambda i, f: (i, 0)), vec, cvec, cvec, cvec,
                pl.BlockSpec((D_MODEL, tf), lambda i, f: (0, f)),
                pl.BlockSpec((tf, D_MODEL), lambda i, f: (f, 0))]
    args = [x2, nw, sc, sh, gate, w1, w2]
    if final_w is not None:
        in_specs.append(vec)
        args.append(final_w)
    return pl.pallas_call(
        functools.partial(_ffn_kernel, final=final_w is not None),
        grid=(m // tm, D_FF // tf),
        in_specs=in_specs,
        out_specs=pl.BlockSpec((tm, D_MODEL), lambda i, f: (i, 0)),
        out_shape=jax.ShapeDtypeStruct((m, D_MODEL), F32),
        scratch_shapes=[pltpu.VMEM((tm, D_MODEL), BF16), pltpu.VMEM((tm, D_MODEL), F32)],
        compiler_params=_params(("arbitrary", "arbitrary")),
        name="ffn",
    )(*args)


def _rope_tables(seq):
    pos = jnp.arange(seq)
    row = (pos // GRID_W).astype(F32)
    col = (pos % GRID_W).astype(F32)
    half = DK_RET // 2
    inv = 1.0 / (ROPE_BASE ** (jnp.arange(0, half, 2, dtype=F32) / half))
    ang = jnp.concatenate([row[:, None] * inv, col[:, None] * inv], -1)
    cs, sn = jnp.cos(ang), jnp.sin(ang)
    return jnp.concatenate([cs, cs], -1), jnp.concatenate([-sn, sn], -1)


def _quad_lanes(v):
    pad = jnp.zeros((DT_LANES - 2 * HQ,), v.dtype)
    return jnp.stack([jnp.concatenate([v[0, q * HQ:(q + 1) * HQ], v[1, q * HQ:(q + 1) * HQ], pad])
                      for q in range(NQ)])[:, None, :]


def _dt_weight(w_dt):
    pad = jnp.zeros((D_MODEL, DT_LANES - 2 * HQ), w_dt.dtype)
    parts = []
    for q in range(NQ):
        parts += [w_dt[:, q * HQ:(q + 1) * HQ], w_dt[:, H_SSD + q * HQ:H_SSD + (q + 1) * HQ], pad]
    return jnp.concatenate(parts, axis=1)


def kernel(x_prompt, x_sample, state_ret, state_ssd, c, c_ctx, w_ada, b_ada, norm1_w, w_in,
           ret_log_decay, conv_w, conv_b, dt_bias, a_log, d_skip, ssd_norm_w, w_out, norm2_w,
           w_ff1, w_ff2, final_norm_w):
    bp, sp, _ = x_prompt.shape
    bs, ss, _ = x_sample.shape

    cond = jnp.zeros((COND_ROWS, D_MODEL), F32).at[:bs].set(c).at[bs].set(c_ctx)
    mod = _ada(cond, w_ada, b_ada).reshape(DEPTH, COND_ROWS, N_MOD, D_MODEL)

    rope = _rope_tables(ss)
    final_w = final_norm_w.reshape(1, D_MODEL)
    w_main_b = [w_in[l, :, :D_MAIN].astype(BF16) for l in range(DEPTH)]
    w_dt_b = [_dt_weight(w_in[l, :, D_MAIN:]).astype(BF16) for l in range(DEPTH)]
    w_out_b = [w_out[l].astype(BF16) for l in range(DEPTH)]
    w_ff1_b = [w_ff1[l].astype(BF16) for l in range(DEPTH)]
    w_ff2_b = [w_ff2[l].astype(BF16) for l in range(DEPTH)]

    def run_group(x, mod_rows, rope_tabs, states, want_state):
        b, seq, _ = x.shape
        x2 = x.reshape(b * seq, D_MODEL)
        rows_per_cond = seq if mod_rows.stop - mod_rows.start > 1 else b * seq
        ret_states, ssd_states = [], []
        for l in range(DEPTH):
            mv = [mod[l, mod_rows, k][:, None, :] for k in range(N_MOD)]
            sh1, sc1, g1, sh2, sc2, g2 = mv
            proj, dt = _in_proj(x2, rows_per_cond, norm1_w[l].reshape(1, D_MODEL), sc1, sh1,
                                w_main_b[l], w_dt_b[l])
            proj3 = proj.reshape(b, seq, D_MAIN)
            dt3 = dt.reshape(b, seq, NQ * DT_LANES)
            init_r = None if states is None else states[0][:, l]
            init_s = None if states is None else states[1][:, l]
            ret = _retention(proj3, ret_log_decay[l], rope_tabs, init_r, want_state)
            ssd = _ssd(proj3, dt3, d_skip[l], _quad_lanes(dt_bias[l]), _quad_lanes(a_log[l]),
                       conv_w[l], conv_b[l].reshape(1, D_XBC), init_s, want_state)
            if want_state:
                ret_states.append(ret[1])
                ssd_states.append(ssd[1])
            x2 = _out_proj(ret[0].reshape(b * seq, D_RET), ssd[0].reshape(b * seq, D_SSD),
                           ssd_norm_w[l].reshape(1, D_SSD), x2, rows_per_cond, g1, w_out_b[l])
            x2 = _ffn(x2, rows_per_cond, norm2_w[l].reshape(1, D_MODEL), sc2, sh2, g2,
                      w_ff1_b[l], w_ff2_b[l], final_w if l == DEPTH - 1 else None)
        return x2.reshape(b, seq, D_MODEL), ret_states, ssd_states

    y_prompt, ret_states, ssd_states = run_group(x_prompt, slice(bs, bs + 1), None, None, True)
    y_sample, _, _ = run_group(x_sample, slice(0, bs), rope, (state_ret, state_ssd), False)
    return (y_prompt, y_sample, jnp.stack(ret_states, 1), jnp.stack(ssd_states, 1))
```

```python
import functools

import jax
import jax.numpy as jnp
from jax import lax
from jax.experimental import pallas as pl
from jax.experimental.pallas import tpu as pltpu

F32 = jnp.float32
BF16 = jnp.bfloat16

D_MODEL = 2048
DEPTH = 4
CHUNK = 128
H_RET = 8
DK_RET = 128
DV_RET = 128
D_RET = H_RET * DV_RET
H_SSD = 16
P_SSD = 64
D_SSD = H_SSD * P_SSD
G_SSD = 2
N_SSD = 128
D_CONV = 3
D_XBC = D_SSD + 2 * G_SSD * N_SSD
D_FF = 4 * D_MODEL
N_MOD = 6
EPS = 1e-6
ROPE_BASE = 10000.0
GRID_W = 64

D_MAIN = 4 * D_RET + D_SSD + D_XBC
OFF_Q, OFF_K, OFF_V, OFF_G = 0, D_RET, 2 * D_RET, 3 * D_RET
OFF_Z = 4 * D_RET
OFF_XS = OFF_Z + D_SSD
OFF_B = OFF_XS + D_SSD
OFF_C = OFF_B + G_SSD * N_SSD

HQ = 4
NQ = H_SSD // HQ
HP = HQ // 2
DT_LANES = 128
COND_ROWS = 16
SCAN_UNROLL = 8
SSD_UNROLL = 2
V7X_VMEM_LIMIT = 56 * 1024 * 1024


def _silu(x):
    return x * jax.nn.sigmoid(x)


def _softplus(x):
    return jnp.maximum(x, 0.0) + jnp.log1p(jnp.exp(-jnp.abs(x)))


def _rms(x, w):
    return x * lax.rsqrt(jnp.mean(x * x, axis=-1, keepdims=True) + EPS) * w


def _params(sem):
    return pltpu.CompilerParams(dimension_semantics=sem, vmem_limit_bytes=V7X_VMEM_LIMIT)


def _ada_kernel(cond_ref, w_ref, b_ref, o_ref):
    s = _silu(cond_ref[...]).astype(BF16)
    o_ref[...] = jnp.dot(s, w_ref[...].astype(BF16), preferred_element_type=F32) + b_ref[...]


def _ada(cond, w_ada, b_ada):
    tn = 1024
    n = N_MOD * D_MODEL
    return pl.pallas_call(
        _ada_kernel,
        grid=(DEPTH, n // tn),
        in_specs=[
            pl.BlockSpec((COND_ROWS, D_MODEL), lambda l, j: (0, 0)),
            pl.BlockSpec((None, D_MODEL, tn), lambda l, j: (l, 0, j)),
            pl.BlockSpec((None, 1, tn), lambda l, j: (l, 0, j)),
        ],
        out_specs=pl.BlockSpec((None, COND_ROWS, tn), lambda l, j: (l, 0, j)),
        out_shape=jax.ShapeDtypeStruct((DEPTH, COND_ROWS, n), F32),
        compiler_params=_params(("arbitrary", "arbitrary")),
        name="ada_mod",
    )(cond, w_ada, b_ada.reshape(DEPTH, 1, n))


def _in_kernel(x_ref, nw_ref, sc_ref, sh_ref, w_ref, wdt_ref, o_ref, dt_ref, h_scr):
    @pl.when(pl.program_id(1) == 0)
    def _():
        h = _rms(x_ref[...], nw_ref[...]) * (1.0 + sc_ref[...]) + sh_ref[...]
        hb = h.astype(BF16)
        h_scr[...] = hb
        dt_ref[...] = jnp.dot(hb, wdt_ref[...], preferred_element_type=F32)

    o_ref[...] = jnp.dot(h_scr[...], w_ref[...], preferred_element_type=F32).astype(o_ref.dtype)


def _in_proj(x2, rows_per_cond, nw, sc, sh, w_in_all, layer, w_dt):
    m = x2.shape[0]
    tm, tn = 1024, 512
    cond_map = lambda i, j: ((i * tm) // rows_per_cond, 0, 0)
    return pl.pallas_call(
        _in_kernel,
        grid=(m // tm, D_MAIN // tn),
        in_specs=[
            pl.BlockSpec((tm, D_MODEL), lambda i, j: (i, 0)),
            pl.BlockSpec((1, D_MODEL), lambda i, j: (0, 0)),
            pl.BlockSpec((None, 1, D_MODEL), cond_map),
            pl.BlockSpec((None, 1, D_MODEL), cond_map),
            pl.BlockSpec((None, D_MODEL, tn), lambda i, j: (layer, 0, j)),
            pl.BlockSpec((D_MODEL, NQ * DT_LANES), lambda i, j: (0, 0)),
        ],
        out_specs=[
            pl.BlockSpec((tm, tn), lambda i, j: (i, j)),
            pl.BlockSpec((tm, NQ * DT_LANES), lambda i, j: (i, 0)),
        ],
        out_shape=[
            jax.ShapeDtypeStruct((m, D_MAIN), BF16),
            jax.ShapeDtypeStruct((m, NQ * DT_LANES), F32),
        ],
        scratch_shapes=[pltpu.VMEM((tm, D_MODEL), BF16)],
        compiler_params=_params(("arbitrary", "arbitrary")),
        name="in_proj",
    )(x2, nw, sc, sh, w_in_all, w_dt)


def _ret_kernel(*refs, seq, rope, has_init, want_state):
    it = iter(refs)
    ld_ref, q_ref, k_ref, v_ref, g_ref = (next(it) for _ in range(5))
    cos_ref = sin_ref = init_ref = st_ref = None
    if rope:
        cos_ref, sin_ref = next(it), next(it)
    if has_init:
        init_ref = next(it)
    y_ref = next(it)
    if want_state:
        st_ref = next(it)
    q_scr, kt_scr, sf_scr, sb_scr = next(it), next(it), next(it), next(it)

    t = CHUNK
    nc = seq // t
    head = pl.program_id(1)
    la_f = ld_ref[0, head]
    la_b = ld_ref[1, head]

    ii = lax.broadcasted_iota(jnp.int32, (t, t), 0)
    jj = lax.broadcasted_iota(jnp.int32, (t, t), 1)
    dist = (ii - jj).astype(F32)
    lower = ii >= jj
    upper = ii <= jj
    dmat = (jnp.where(lower, jnp.exp(la_f * jnp.where(lower, dist, 0.0)), 0.0)
            + jnp.where(upper, jnp.exp(la_b * jnp.where(upper, -dist, 0.0)), 0.0))
    pos = lax.broadcasted_iota(jnp.int32, (t, 1), 0).astype(F32)
    lane = lax.broadcasted_iota(jnp.int32, (1, t), 1).astype(F32)
    e_f = jnp.exp(la_f * (pos + 1.0))
    e_b = jnp.exp(la_b * (t - pos))
    w_f = jnp.exp(la_f * (t - 1.0 - lane))
    w_b = jnp.exp(la_b * lane)
    dec_f = jnp.exp(jnp.full((1, 1), t, F32) * la_f)
    dec_b = jnp.exp(jnp.full((1, 1), t, F32) * la_b)
    scale = DK_RET ** -0.5
    unroll = min(nc, SCAN_UNROLL)

    def rows(c):
        return pl.ds(pl.multiple_of(c * t, t), t)

    def prep(c, carry):
        r = rows(c)
        q = q_ref[r, :].astype(F32)
        k = k_ref[r, :].astype(F32)
        if rope:
            cs, sn = cos_ref[r, :], sin_ref[r, :]
            q = q * cs + pltpu.roll(q, DK_RET // 2, 1) * sn
            k = k * cs + pltpu.roll(k, DK_RET // 2, 1) * sn
        q_scr[r, :] = (q * scale).astype(BF16)
        kt_scr[c] = k.T.astype(BF16)
        return carry

    lax.fori_loop(0, nc, prep, 0, unroll=unroll)

    def chunk_state(c, w_row):
        kw = (kt_scr[c].astype(F32) * w_row).astype(BF16)
        return jnp.dot(kw, v_ref[rows(c), :], preferred_element_type=F32)

    def states(i, carry):
        s_f, s_b = carry
        c_f, c_b = i, nc - 1 - i
        sf_scr[c_f] = s_f.astype(BF16)
        sb_scr[c_b] = s_b.astype(BF16)
        return dec_f * s_f + chunk_state(c_f, w_f), dec_b * s_b + chunk_state(c_b, w_b)

    def outs(c, carry):
        r = rows(c)
        q = q_scr[r, :]
        att = jnp.dot(q, kt_scr[c], preferred_element_type=F32)
        qf = q.astype(F32)
        a = jnp.concatenate([(att * dmat).astype(BF16), (qf * e_f).astype(BF16),
                             (qf * e_b).astype(BF16)], axis=1)
        rhs = jnp.concatenate([v_ref[r, :], sf_scr[c], sb_scr[c]], axis=0)
        y = jnp.dot(a, rhs, preferred_element_type=F32)
        yn = y * lax.rsqrt(jnp.mean(y * y, axis=-1, keepdims=True) + EPS)
        y_ref[r, :] = (_silu(g_ref[r, :].astype(F32)) * yn).astype(y_ref.dtype)
        return carry

    if has_init:
        s0_f, s0_b = init_ref[0].T, init_ref[1].T
    else:
        s0_f = s0_b = jnp.zeros((DK_RET, DV_RET), F32)
    s_f, s_b = lax.fori_loop(0, nc, states, (s0_f, s0_b), unroll=unroll)
    lax.fori_loop(0, nc, outs, 0, unroll=unroll)
    if want_state:
        st_ref[0] = s_f.T
        st_ref[1] = s_b.T


def _retention(proj3, ret_ld, rope_tabs, init, layer, want_state):
    b, seq, _ = proj3.shape
    hblk = lambda off: pl.BlockSpec((None, seq, DK_RET), lambda i, h, o=off // DK_RET: (i, 0, o + h))
    in_specs = [pl.BlockSpec(memory_space=pltpu.SMEM), hblk(OFF_Q), hblk(OFF_K), hblk(OFF_V), hblk(OFF_G)]
    args = [ret_ld, proj3, proj3, proj3, proj3]
    if rope_tabs is not None:
        in_specs += [pl.BlockSpec((seq, DK_RET), lambda i, h: (0, 0))] * 2
        args += list(rope_tabs)
    st_spec = pl.BlockSpec((None, 2, None, DV_RET, DK_RET), lambda i, h: (i, 0, h, 0, 0))
    if init is not None:
        in_specs.append(pl.BlockSpec((None, None, 2, None, DV_RET, DK_RET),
                                     lambda i, h: (i, layer, 0, h, 0, 0)))
        args.append(init)
    out_specs = [pl.BlockSpec((None, seq, DV_RET), lambda i, h: (i, 0, h))]
    out_shape = [jax.ShapeDtypeStruct((b, seq, D_RET), BF16)]
    if want_state:
        out_specs.append(st_spec)
        out_shape.append(jax.ShapeDtypeStruct((b, 2, H_RET, DV_RET, DK_RET), F32))
    return pl.pallas_call(
        functools.partial(_ret_kernel, seq=seq, rope=rope_tabs is not None,
                          has_init=init is not None, want_state=want_state),
        grid=(b, H_RET),
        in_specs=in_specs,
        out_specs=out_specs,
        out_shape=out_shape,
        scratch_shapes=[pltpu.VMEM((seq, DK_RET), BF16),
                        pltpu.VMEM((seq // CHUNK, DK_RET, CHUNK), BF16),
                        pltpu.VMEM((seq // CHUNK, DK_RET, DV_RET), BF16),
                        pltpu.VMEM((seq // CHUNK, DK_RET, DV_RET), BF16)],
        compiler_params=_params(("arbitrary", "arbitrary")),
        name="retention",
    )(*args)


def _conv_silu_chunk(src_ref, w_ref, b_ref, c, nc):
    t = CHUNK
    r0 = pl.multiple_of(c * t, t)
    cur = src_ref[pl.ds(r0, t), :].astype(F32)
    p0 = pl.multiple_of(jnp.maximum(r0 - 16, 0), 16)
    n0 = pl.multiple_of(jnp.minimum(r0 + t, nc * t - 16), 16)
    prev = jnp.where(c > 0, src_ref[pl.ds(p0, 16), :].astype(F32)[15:16, :], 0.0)
    nxt = jnp.where(c < nc - 1, src_ref[pl.ds(n0, 16), :].astype(F32)[0:1, :], 0.0)
    row = lax.broadcasted_iota(jnp.int32, (t, 1), 0)
    xm1 = jnp.where(row == 0, prev, pltpu.roll(cur, 1, 0))
    xp1 = jnp.where(row == t - 1, nxt, pltpu.roll(cur, t - 1, 0))
    return _silu(xm1 * w_ref[0:1, :] + cur * w_ref[1:2, :] + xp1 * w_ref[2:3, :] + b_ref[...])


def _exact_cumsum(tri_b, x):
    hi = x.astype(BF16)
    r1 = x - hi.astype(F32)
    mid = r1.astype(BF16)
    lo = (r1 - mid.astype(F32)).astype(BF16)
    return (jnp.dot(tri_b, hi, preferred_element_type=F32)
            + jnp.dot(tri_b, mid, preferred_element_type=F32)
            + jnp.dot(tri_b, lo, preferred_element_type=F32))


def _ssd_kernel(*refs, seq, has_init, want_state):
    it = iter(refs)
    (dsk_ref, xs_ref, z_ref, b_ref, c_ref, dt_ref, bias_ref, alog_ref,
     wx_ref, bx_ref, wb_ref, bb_ref, wc_ref, bc_ref) = (next(it) for _ in range(14))
    init_ref = st_ref = None
    if has_init:
        init_ref = next(it)
    u_ref = next(it)
    if want_state:
        st_ref = next(it)
    xc_scr, bt_scr, cc_scr, col_scr, row_scr, sf_scr, sb_scr = (next(it) for _ in range(7))

    t = CHUNK
    nc = seq // t
    quad = pl.program_id(1)
    unroll = min(nc, SSD_UNROLL)

    ii = lax.broadcasted_iota(jnp.int32, (t, t), 0)
    jj = lax.broadcasted_iota(jnp.int32, (t, t), 1)
    lower = ii >= jj
    strict_lower = ii > jj
    diag = ii == jj
    tril_b = jnp.where(lower, 1.0, 0.0).astype(BF16)
    lane = lax.broadcasted_iota(jnp.int32, (1, DT_LANES), 1)
    a_neg = -jnp.exp(alog_ref[...])
    dt_bias = bias_ref[...]

    def rows(c):
        return pl.ds(pl.multiple_of(c * t, t), t)

    def prep(c, carry):
        r = rows(c)
        xc_scr[r, :] = _conv_silu_chunk(xs_ref, wx_ref, bx_ref, c, nc).astype(BF16)
        bt_scr[c] = _conv_silu_chunk(b_ref, wb_ref, bb_ref, c, nc).T.astype(BF16)
        cc_scr[r, :] = _conv_silu_chunk(c_ref, wc_ref, bc_ref, c, nc).astype(BF16)
        dt = _softplus(dt_ref[r, :] + dt_bias)
        da = dt * a_neg
        cum = _exact_cumsum(tril_b, da)
        total = cum[t - 1:t, :]
        decay = jnp.where(lane < HQ, cum, total - cum + da)
        packed = jnp.where(lane < 2 * HQ, decay, dt)
        col_scr[r, :] = packed
        row_scr[c] = packed.T[0:4 * HQ, :]
        return carry

    lax.fori_loop(0, nc, prep, 0, unroll=unroll)

    even = lax.broadcasted_iota(jnp.int32, (1, 2 * P_SSD), 1) < P_SSD

    def pair_cols(k):
        return slice(k * 2 * P_SSD, (k + 1) * 2 * P_SSD)

    def chunk_states(c, first, tot_lane):
        rowf = row_scr[c]
        cum_r = rowf[first:first + HQ, :]
        dt_r = rowf[2 * HQ + first:3 * HQ + first, :]
        tot = cum_r[:, tot_lane:tot_lane + 1]
        w = jnp.exp(tot - cum_r) * dt_r
        dec = jnp.exp(tot)
        btf = bt_scr[c].astype(F32)
        r = rows(c)
        decs, css = [], []
        for k in range(HP):
            x_pair = xc_scr[r, pair_cols(k)]
            cs = [jnp.dot((btf * w[hh:hh + 1, :]).astype(BF16), x_pair, preferred_element_type=F32)
                  for hh in (2 * k, 2 * k + 1)]
            css.append(jnp.where(even, cs[0], cs[1]))
            decs.append(jnp.where(even, dec[2 * k:2 * k + 1, :], dec[2 * k + 1:2 * k + 2, :]))
        return decs, css

    def states(i, carry):
        s_f, s_b = carry
        c_f, c_b = i, nc - 1 - i
        dec_f, cs_f = chunk_states(c_f, 0, t - 1)
        dec_b, cs_b = chunk_states(c_b, HQ, 0)
        new_f, new_b = [], []
        for k in range(HP):
            sf_scr[c_f * HP + k] = s_f[k].astype(BF16)
            sb_scr[c_b * HP + k] = s_b[k].astype(BF16)
            new_f.append(dec_f[k] * s_f[k] + cs_f[k])
            new_b.append(dec_b[k] * s_b[k] + cs_b[k])
        return tuple(new_f), tuple(new_b)

    def head_lhs(colf, rowf, cb, cf, hh):
        fl, bl = hh, HQ + hh
        rc_f = jnp.broadcast_to(colf[:, fl:fl + 1], (t, t))
        rc_b = jnp.broadcast_to(colf[:, bl:bl + 1], (t, t))
        seg = jnp.where(lower, rc_f - rowf[fl:fl + 1, :], rc_b - rowf[bl:bl + 1, :])
        dt_f = rowf[2 * HQ + fl:2 * HQ + fl + 1, :]
        dt_b = rowf[2 * HQ + bl:2 * HQ + bl + 1, :]
        dt_sel = jnp.where(strict_lower, dt_f, jnp.where(diag, dt_f + dt_b, dt_b))
        return jnp.concatenate([(cb * (jnp.exp(seg) * dt_sel)).astype(BF16),
                                (cf * jnp.exp(rc_f)).astype(BF16),
                                (cf * jnp.exp(rc_b)).astype(BF16)], axis=1)

    def outs(c, carry):
        r = rows(c)
        colf = col_scr[r, :]
        rowf = row_scr[c]
        cmat = cc_scr[r, :]
        cf = cmat.astype(F32)
        cb = jnp.dot(cmat, bt_scr[c], preferred_element_type=F32)
        for k in range(HP):
            x_pair = xc_scr[r, pair_cols(k)]
            rhs = jnp.concatenate([x_pair, sf_scr[c * HP + k], sb_scr[c * HP + k]], axis=0)
            ys = [jnp.dot(head_lhs(colf, rowf, cb, cf, hh), rhs, preferred_element_type=F32)
                  for hh in (2 * k, 2 * k + 1)]
            h0 = quad * HQ + 2 * k
            skip = jnp.where(even, dsk_ref[h0], dsk_ref[h0 + 1])
            y = jnp.where(even, ys[0], ys[1]) + skip * x_pair.astype(F32)
            u_ref[r, pair_cols(k)] = (y * _silu(z_ref[r, pair_cols(k)].astype(F32))).astype(u_ref.dtype)
        return carry

    def pair_state(ref, d, k):
        return ref[d, 2 * k:2 * k + 2].reshape(2 * P_SSD, N_SSD).T

    if has_init:
        s0_f = tuple(pair_state(init_ref, 0, k) for k in range(HP))
        s0_b = tuple(pair_state(init_ref, 1, k) for k in range(HP))
    else:
        s0_f = s0_b = tuple(jnp.zeros((N_SSD, 2 * P_SSD), F32) for _ in range(HP))
    s_f, s_b = lax.fori_loop(0, nc, states, (s0_f, s0_b), unroll=unroll)
    lax.fori_loop(0, nc, outs, 0, unroll=unroll)
    if want_state:
        for k in range(HP):
            st_ref[0, 2 * k:2 * k + 2] = s_f[k].T.reshape(2, P_SSD, N_SSD)
            st_ref[1, 2 * k:2 * k + 2] = s_b[k].T.reshape(2, P_SSD, N_SSD)


def _ssd(proj3, dt3, d_skip, dt_bias_q, a_log_q, conv_w, conv_b, init, layer, want_state):
    b, seq, _ = proj3.shape
    wq = HQ * P_SSD
    col = lambda width, off, fn: pl.BlockSpec((None, seq, width),
                                              lambda i, q, o=off // width: (i, 0, o + fn(q)))
    ident = lambda q: q
    group = lambda q: q // (NQ // G_SSD)
    xoff = 0
    boff = D_SSD
    coff = D_SSD + G_SSD * N_SSD
    cw = lambda width, off, fn: pl.BlockSpec((D_CONV, width), lambda i, q, o=off // width: (0, o + fn(q)))
    cbias = lambda width, off, fn: pl.BlockSpec((1, width), lambda i, q, o=off // width: (0, o + fn(q)))
    in_specs = [
        pl.BlockSpec(memory_space=pltpu.SMEM),
        col(wq, OFF_XS, ident), col(wq, OFF_Z, ident), col(N_SSD, OFF_B, group), col(N_SSD, OFF_C, group),
        pl.BlockSpec((None, seq, DT_LANES), lambda i, q: (i, 0, q)),
        pl.BlockSpec((None, 1, DT_LANES), lambda i, q: (q, 0, 0)),
        pl.BlockSpec((None, 1, DT_LANES), lambda i, q: (q, 0, 0)),
        cw(wq, xoff, ident), cbias(wq, xoff, ident),
        cw(N_SSD, boff, group), cbias(N_SSD, boff, group),
        cw(N_SSD, coff, group), cbias(N_SSD, coff, group),
    ]
    args = [d_skip, proj3, proj3, proj3, proj3, dt3, dt_bias_q, a_log_q,
            conv_w, conv_b, conv_w, conv_b, conv_w, conv_b]
    st_spec = pl.BlockSpec((None, 2, HQ, P_SSD, N_SSD), lambda i, q: (i, 0, q, 0, 0))
    if init is not None:
        in_specs.append(pl.BlockSpec((None, None, 2, HQ, P_SSD, N_SSD),
                                     lambda i, q: (i, layer, 0, q, 0, 0)))
        args.append(init)
    out_specs = [pl.BlockSpec((None, seq, wq), lambda i, q: (i, 0, q))]
    out_shape = [jax.ShapeDtypeStruct((b, seq, D_SSD), BF16)]
    if want_state:
        out_specs.append(st_spec)
        out_shape.append(jax.ShapeDtypeStruct((b, 2, H_SSD, P_SSD, N_SSD), F32))
    nc = seq // CHUNK
    return pl.pallas_call(
        functools.partial(_ssd_kernel, seq=seq, has_init=init is not None, want_state=want_state),
        grid=(b, NQ),
        in_specs=in_specs,
        out_specs=out_specs,
        out_shape=out_shape,
        scratch_shapes=[
            pltpu.VMEM((seq, wq), BF16), pltpu.VMEM((nc, N_SSD, CHUNK), BF16), pltpu.VMEM((seq, N_SSD), BF16),
            pltpu.VMEM((seq, DT_LANES), F32), pltpu.VMEM((nc, 4 * HQ, CHUNK), F32),
            pltpu.VMEM((nc * HP, N_SSD, 2 * P_SSD), BF16), pltpu.VMEM((nc * HP, N_SSD, 2 * P_SSD), BF16),
        ],
        compiler_params=_params(("arbitrary", "arbitrary")),
        name="ssd",
    )(*args)


def _out_kernel(yr_ref, us_ref, nw_ref, x_ref, g_ref, w_ref, o_ref, a_scr):
    @pl.when(pl.program_id(1) == 0)
    def _():
        a_scr[:, :D_RET] = yr_ref[...]
        a_scr[:, D_RET:] = _rms(us_ref[...].astype(F32), nw_ref[...]).astype(BF16)

    acc = jnp.dot(a_scr[...], w_ref[...], preferred_element_type=F32)
    o_ref[...] = x_ref[...] + g_ref[...] * acc


def _out_proj(yr, us, ssd_nw, x2, rows_per_cond, gate, w_out_all, layer):
    m = x2.shape[0]
    tm, tn = 1024, 1024
    return pl.pallas_call(
        _out_kernel,
        grid=(m // tm, D_MODEL // tn),
        in_specs=[
            pl.BlockSpec((tm, D_RET), lambda i, j: (i, 0)),
            pl.BlockSpec((tm, D_SSD), lambda i, j: (i, 0)),
            pl.BlockSpec((1, D_SSD), lambda i, j: (0, 0)),
            pl.BlockSpec((tm, tn), lambda i, j: (i, j)),
            pl.BlockSpec((None, 1, tn), lambda i, j: ((i * tm) // rows_per_cond, 0, j)),
            pl.BlockSpec((None, D_RET + D_SSD, tn), lambda i, j: (layer, 0, j)),
        ],
        out_specs=pl.BlockSpec((tm, tn), lambda i, j: (i, j)),
        out_shape=jax.ShapeDtypeStruct((m, D_MODEL), F32),
        scratch_shapes=[pltpu.VMEM((tm, D_RET + D_SSD), BF16)],
        compiler_params=_params(("arbitrary", "arbitrary")),
        name="out_proj",
    )(yr, us, ssd_nw, x2, gate, w_out_all)


def _ffn_kernel(*refs, final):
    it = iter(refs)
    x_ref, nw_ref, sc_ref, sh_ref, g_ref, w1_ref, w2_ref = (next(it) for _ in range(7))
    fw_ref = next(it) if final else None
    o_ref, h_scr, acc_scr = next(it), next(it), next(it)
    f = pl.program_id(1)

    @pl.when(f == 0)
    def _():
        h = _rms(x_ref[...], nw_ref[...]) * (1.0 + sc_ref[...]) + sh_ref[...]
        h_scr[...] = h.astype(BF16)

    a = jnp.maximum(jnp.dot(h_scr[...], w1_ref[...], preferred_element_type=F32), 0.0)
    contrib = jnp.dot((a * a).astype(BF16), w2_ref[...], preferred_element_type=F32)

    @pl.when(f == 0)
    def _():
        acc_scr[...] = contrib

    @pl.when(f > 0)
    def _():
        acc_scr[...] += contrib

    @pl.when(f == pl.num_programs(1) - 1)
    def _():
        y = x_ref[...] + g_ref[...] * acc_scr[...]
        if final:
            y = _rms(y, fw_ref[...])
        o_ref[...] = y


def _ffn(x2, rows_per_cond, nw, sc, sh, gate, w1, w2, layer, final_w):
    m = x2.shape[0]
    tm, tf = 512, 1024
    cond_map = lambda i, f: ((i * tm) // rows_per_cond, 0, 0)
    vec = pl.BlockSpec((1, D_MODEL), lambda i, f: (0, 0))
    cvec = pl.BlockSpec((None, 1, D_MODEL), cond_map)
    in_specs = [pl.BlockSpec((tm, D_MODEL), lambda i, f: (i, 0)), vec, cvec, cvec, cvec,
                pl.BlockSpec((None, D_MODEL, tf), lambda i, f: (layer, 0, f)),
                pl.BlockSpec((None, tf, D_MODEL), lambda i, f: (layer, f, 0))]
    args = [x2, nw, sc, sh, gate, w1, w2]
    if final_w is not None:
        in_specs.append(vec)
        args.append(final_w)
    return pl.pallas_call(
        functools.partial(_ffn_kernel, final=final_w is not None),
        grid=(m // tm, D_FF // tf),
        in_specs=in_specs,
        out_specs=pl.BlockSpec((tm, D_MODEL), lambda i, f: (i, 0)),
        out_shape=jax.ShapeDtypeStruct((m, D_MODEL), F32),
        scratch_shapes=[pltpu.VMEM((tm, D_MODEL), BF16), pltpu.VMEM((tm, D_MODEL), F32)],
        compiler_params=_params(("arbitrary", "arbitrary")),
        name="ffn",
    )(*args)


def _rope_tables(seq):
    pos = jnp.arange(seq)
    row = (pos // GRID_W).astype(F32)
    col = (pos % GRID_W).astype(F32)
    half = DK_RET // 2
    inv = 1.0 / (ROPE_BASE ** (jnp.arange(0, half, 2, dtype=F32) / half))
    ang = jnp.concatenate([row[:, None] * inv, col[:, None] * inv], -1)
    cs, sn = jnp.cos(ang), jnp.sin(ang)
    return jnp.concatenate([cs, cs], -1), jnp.concatenate([-sn, sn], -1)


def _quad_lanes(v):
    pad = jnp.zeros((DT_LANES - 4 * HQ,), v.dtype)
    quads = []
    for q in range(NQ):
        fb = [v[0, q * HQ:(q + 1) * HQ], v[1, q * HQ:(q + 1) * HQ]]
        quads.append(jnp.concatenate(fb + fb + [pad]))
    return jnp.stack(quads)[:, None, :]


def _dt_weight(w_dt):
    pad = jnp.zeros((D_MODEL, DT_LANES - 4 * HQ), w_dt.dtype)
    parts = []
    for q in range(NQ):
        fb = [w_dt[:, q * HQ:(q + 1) * HQ], w_dt[:, H_SSD + q * HQ:H_SSD + (q + 1) * HQ]]
        parts += fb + fb + [pad]
    return jnp.concatenate(parts, axis=1)


def kernel(x_prompt, x_sample, state_ret, state_ssd, c, c_ctx, w_ada, b_ada, norm1_w, w_in,
           ret_log_decay, conv_w, conv_b, dt_bias, a_log, d_skip, ssd_norm_w, w_out, norm2_w,
           w_ff1, w_ff2, final_norm_w):
    bp, sp, _ = x_prompt.shape
    bs, ss, _ = x_sample.shape

    cond = jnp.zeros((COND_ROWS, D_MODEL), F32).at[:bs].set(c).at[bs].set(c_ctx)
    mod = _ada(cond, w_ada, b_ada).reshape(DEPTH, COND_ROWS, N_MOD, D_MODEL)

    rope = _rope_tables(ss)
    final_w = final_norm_w.reshape(1, D_MODEL)
    w_in_b = w_in.astype(BF16)
    w_dt_b = [_dt_weight(w_in[l, :, D_MAIN:]).astype(BF16) for l in range(DEPTH)]
    w_out_b = w_out.astype(BF16)
    w_ff1_b = w_ff1.astype(BF16)
    w_ff2_b = w_ff2.astype(BF16)

    def run_group(x, mod_rows, rope_tabs, states, want_state):
        b, seq, _ = x.shape
        x2 = x.reshape(b * seq, D_MODEL)
        rows_per_cond = seq if mod_rows.stop - mod_rows.start > 1 else b * seq
        ret_states, ssd_states = [], []
        for l in range(DEPTH):
            mv = [mod[l, mod_rows, k][:, None, :] for k in range(N_MOD)]
            sh1, sc1, g1, sh2, sc2, g2 = mv
            proj, dt = _in_proj(x2, rows_per_cond, norm1_w[l].reshape(1, D_MODEL), sc1, sh1,
                                w_in_b, l, w_dt_b[l])
            proj3 = proj.reshape(b, seq, D_MAIN)
            dt3 = dt.reshape(b, seq, NQ * DT_LANES)
            init_r, init_s = (None, None) if states is None else states
            ret = _retention(proj3, ret_log_decay[l], rope_tabs, init_r, l, want_state)
            ssd = _ssd(proj3, dt3, d_skip[l], _quad_lanes(dt_bias[l]), _quad_lanes(a_log[l]),
                       conv_w[l], conv_b[l].reshape(1, D_XBC), init_s, l, want_state)
            if want_state:
                ret_states.append(ret[1])
                ssd_states.append(ssd[1])
            x2 = _out_proj(ret[0].reshape(b * seq, D_RET), ssd[0].reshape(b * seq, D_SSD),
                           ssd_norm_w[l].reshape(1, D_SSD), x2, rows_per_cond, g1, w_out_b, l)
            x2 = _ffn(x2, rows_per_cond, norm2_w[l].reshape(1, D_MODEL), sc2, sh2, g2,
                      w_ff1_b, w_ff2_b, l, final_w if l == DEPTH - 1 else None)
        return x2.reshape(b, seq, D_MODEL), ret_states, ssd_states

    y_prompt, ret_states, ssd_states = run_group(x_prompt, slice(bs, bs + 1), None, None, True)
    y_sample, _, _ = run_group(x_sample, slice(0, bs), rope, (state_ret, state_ssd), False)
    return (y_prompt, y_sample, jnp.stack(ret_states, 1), jnp.stack(ssd_states, 1))
```

```python
import functools

import jax
import jax.numpy as jnp
from jax import lax
from jax.experimental import pallas as pl
from jax.experimental.pallas import tpu as pltpu

F32 = jnp.float32
BF16 = jnp.bfloat16

D_MODEL = 2048
DEPTH = 4
CHUNK = 128
H_RET = 8
DK_RET = 128
DV_RET = 128
D_RET = H_RET * DV_RET
H_SSD = 16
P_SSD = 64
D_SSD = H_SSD * P_SSD
G_SSD = 2
N_SSD = 128
D_CONV = 3
D_XBC = D_SSD + 2 * G_SSD * N_SSD
D_FF = 4 * D_MODEL
N_MOD = 6
EPS = 1e-6
ROPE_BASE = 10000.0
GRID_W = 64

D_MAIN = 4 * D_RET + D_SSD + D_XBC
OFF_Q, OFF_K, OFF_V, OFF_G = 0, D_RET, 2 * D_RET, 3 * D_RET
OFF_Z = 4 * D_RET
OFF_XS = OFF_Z + D_SSD
OFF_B = OFF_XS + D_SSD
OFF_C = OFF_B + G_SSD * N_SSD

HQ = 4
NQ = H_SSD // HQ
HP = HQ // 2
DT_LANES = 128
QUAD_LANES = 4 * HQ
COND_ROWS = 16
SCAN_UNROLL = 8
SSD_UNROLL = 2
V7X_VMEM_LIMIT = 56 * 1024 * 1024


def _silu(x):
    return x * jax.nn.sigmoid(x)


def _softplus(x):
    return jnp.maximum(x, 0.0) + jnp.log1p(jnp.exp(-jnp.abs(x)))


def _rms(x, w):
    return x * lax.rsqrt(jnp.mean(x * x, axis=-1, keepdims=True) + EPS) * w


def _params(sem):
    return pltpu.CompilerParams(dimension_semantics=sem, vmem_limit_bytes=V7X_VMEM_LIMIT)


def _ada_kernel(cond_ref, w_ref, b_ref, o_ref):
    s = _silu(cond_ref[...]).astype(BF16)
    o_ref[...] = jnp.dot(s, w_ref[...].astype(BF16), preferred_element_type=F32) + b_ref[...]


def _ada(cond, w_ada, b_ada):
    tn = 1024
    n = N_MOD * D_MODEL
    return pl.pallas_call(
        _ada_kernel,
        grid=(DEPTH, n // tn),
        in_specs=[
            pl.BlockSpec((COND_ROWS, D_MODEL), lambda l, j: (0, 0)),
            pl.BlockSpec((None, D_MODEL, tn), lambda l, j: (l, 0, j)),
            pl.BlockSpec((None, 1, tn), lambda l, j: (l, 0, j)),
        ],
        out_specs=pl.BlockSpec((None, COND_ROWS, tn), lambda l, j: (l, 0, j)),
        out_shape=jax.ShapeDtypeStruct((DEPTH, COND_ROWS, n), F32),
        compiler_params=_params(("arbitrary", "arbitrary")),
        name="ada_mod",
    )(cond, w_ada, b_ada.reshape(DEPTH, 1, n))


def _in_kernel(x_ref, nw_ref, sc_ref, sh_ref, w_ref, wdt_ref, o_ref, dt_ref, h_scr):
    @pl.when(pl.program_id(1) == 0)
    def _():
        h = _rms(x_ref[...], nw_ref[...]) * (1.0 + sc_ref[...]) + sh_ref[...]
        hb = h.astype(BF16)
        h_scr[...] = hb
        dt = jnp.dot(hb, wdt_ref[...], preferred_element_type=F32)
        for q in range(NQ):
            shift = (DT_LANES - QUAD_LANES * q) % DT_LANES
            dt_ref[:, q * DT_LANES:(q + 1) * DT_LANES] = pltpu.roll(dt, shift, 1) if shift else dt

    o_ref[...] = jnp.dot(h_scr[...], w_ref[...], preferred_element_type=F32).astype(o_ref.dtype)


def _in_proj(x2, rows_per_cond, nw, sc, sh, w_in_all, layer, w_dt):
    m = x2.shape[0]
    tm, tn = 512, D_MAIN // 2
    cond_map = lambda i, j: ((i * tm) // rows_per_cond, 0, 0)
    return pl.pallas_call(
        _in_kernel,
        grid=(m // tm, D_MAIN // tn),
        in_specs=[
            pl.BlockSpec((tm, D_MODEL), lambda i, j: (i, 0)),
            pl.BlockSpec((1, D_MODEL), lambda i, j: (0, 0)),
            pl.BlockSpec((None, 1, D_MODEL), cond_map),
            pl.BlockSpec((None, 1, D_MODEL), cond_map),
            pl.BlockSpec((None, D_MODEL, tn), lambda i, j: (layer, 0, j)),
            pl.BlockSpec((D_MODEL, DT_LANES), lambda i, j: (0, 0)),
        ],
        out_specs=[
            pl.BlockSpec((tm, tn), lambda i, j: (i, j)),
            pl.BlockSpec((tm, NQ * DT_LANES), lambda i, j: (i, 0)),
        ],
        out_shape=[
            jax.ShapeDtypeStruct((m, D_MAIN), BF16),
            jax.ShapeDtypeStruct((m, NQ * DT_LANES), F32),
        ],
        scratch_shapes=[pltpu.VMEM((tm, D_MODEL), BF16)],
        compiler_params=_params(("arbitrary", "arbitrary")),
        name="in_proj",
    )(x2, nw, sc, sh, w_in_all, w_dt)


def _ret_kernel(*refs, seq, rope, has_init, want_state):
    it = iter(refs)
    ld_ref, q_ref, k_ref, v_ref, g_ref = (next(it) for _ in range(5))
    cos_ref = sin_ref = init_ref = st_ref = None
    if rope:
        cos_ref, sin_ref = next(it), next(it)
    if has_init:
        init_ref = next(it)
    y_ref = next(it)
    if want_state:
        st_ref = next(it)
    q_scr, kt_scr, sf_scr, sb_scr = next(it), next(it), next(it), next(it)

    t = CHUNK
    nc = seq // t
    head = pl.program_id(1)
    la_f = ld_ref[0, head]
    la_b = ld_ref[1, head]

    ii = lax.broadcasted_iota(jnp.int32, (t, t), 0)
    jj = lax.broadcasted_iota(jnp.int32, (t, t), 1)
    dist = (ii - jj).astype(F32)
    lower = ii >= jj
    upper = ii <= jj
    dmat = (jnp.where(lower, jnp.exp(la_f * jnp.where(lower, dist, 0.0)), 0.0)
            + jnp.where(upper, jnp.exp(la_b * jnp.where(upper, -dist, 0.0)), 0.0))
    pos = lax.broadcasted_iota(jnp.int32, (t, 1), 0).astype(F32)
    lane = lax.broadcasted_iota(jnp.int32, (1, t), 1).astype(F32)
    e_f = jnp.exp(la_f * (pos + 1.0))
    e_b = jnp.exp(la_b * (t - pos))
    w_f = jnp.exp(la_f * (t - 1.0 - lane))
    w_b = jnp.exp(la_b * lane)
    dec_f = jnp.exp(jnp.full((1, 1), t, F32) * la_f)
    dec_b = jnp.exp(jnp.full((1, 1), t, F32) * la_b)
    scale = DK_RET ** -0.5
    unroll = min(nc, SCAN_UNROLL)

    def rows(c):
        return pl.ds(pl.multiple_of(c * t, t), t)

    def prep(c, carry):
        r = rows(c)
        q = q_ref[r, :].astype(F32)
        k = k_ref[r, :].astype(F32)
        if rope:
            cs, sn = cos_ref[r, :], sin_ref[r, :]
            q = q * cs + pltpu.roll(q, DK_RET // 2, 1) * sn
            k = k * cs + pltpu.roll(k, DK_RET // 2, 1) * sn
        q_scr[r, :] = (q * scale).astype(BF16)
        kt_scr[c] = k.T.astype(BF16)
        return carry

    lax.fori_loop(0, nc, prep, 0, unroll=unroll)

    def chunk_state(c, w_row):
        kw = (kt_scr[c].astype(F32) * w_row).astype(BF16)
        return jnp.dot(kw, v_ref[rows(c), :], preferred_element_type=F32)

    def states(i, carry):
        s_f, s_b = carry
        c_f, c_b = i, nc - 1 - i
        sf_scr[c_f] = s_f.astype(BF16)
        sb_scr[c_b] = s_b.astype(BF16)
        return dec_f * s_f + chunk_state(c_f, w_f), dec_b * s_b + chunk_state(c_b, w_b)

    def outs(c, carry):
        r = rows(c)
        q = q_scr[r, :]
        att = jnp.dot(q, kt_scr[c], preferred_element_type=F32)
        qf = q.astype(F32)
        a = jnp.concatenate([(att * dmat).astype(BF16), (qf * e_f).astype(BF16),
                             (qf * e_b).astype(BF16)], axis=1)
        rhs = jnp.concatenate([v_ref[r, :], sf_scr[c], sb_scr[c]], axis=0)
        y = jnp.dot(a, rhs, preferred_element_type=F32)
        yn = y * lax.rsqrt(jnp.mean(y * y, axis=-1, keepdims=True) + EPS)
        y_ref[r, :] = (_silu(g_ref[r, :].astype(F32)) * yn).astype(y_ref.dtype)
        return carry

    if has_init:
        s0_f, s0_b = init_ref[0].T, init_ref[1].T
    else:
        s0_f = s0_b = jnp.zeros((DK_RET, DV_RET), F32)
    s_f, s_b = lax.fori_loop(0, nc, states, (s0_f, s0_b), unroll=unroll)
    lax.fori_loop(0, nc, outs, 0, unroll=unroll)
    if want_state:
        st_ref[0] = s_f.T
        st_ref[1] = s_b.T


def _retention(proj3, ret_ld, rope_tabs, init, layer, want_state):
    b, seq, _ = proj3.shape
    hblk = lambda off: pl.BlockSpec((None, seq, DK_RET), lambda i, h, o=off // DK_RET: (i, 0, o + h))
    in_specs = [pl.BlockSpec(memory_space=pltpu.SMEM), hblk(OFF_Q), hblk(OFF_K), hblk(OFF_V), hblk(OFF_G)]
    args = [ret_ld, proj3, proj3, proj3, proj3]
    if rope_tabs is not None:
        in_specs += [pl.BlockSpec((seq, DK_RET), lambda i, h: (0, 0))] * 2
        args += list(rope_tabs)
    st_spec = pl.BlockSpec((None, 2, None, DV_RET, DK_RET), lambda i, h: (i, 0, h, 0, 0))
    if init is not None:
        in_specs.append(pl.BlockSpec((None, None, 2, None, DV_RET, DK_RET),
                                     lambda i, h: (i, layer, 0, h, 0, 0)))
        args.append(init)
    out_specs = [pl.BlockSpec((None, seq, DV_RET), lambda i, h: (i, 0, h))]
    out_shape = [jax.ShapeDtypeStruct((b, seq, D_RET), BF16)]
    if want_state:
        out_specs.append(st_spec)
        out_shape.append(jax.ShapeDtypeStruct((b, 2, H_RET, DV_RET, DK_RET), F32))
    return pl.pallas_call(
        functools.partial(_ret_kernel, seq=seq, rope=rope_tabs is not None,
                          has_init=init is not None, want_state=want_state),
        grid=(b, H_RET),
        in_specs=in_specs,
        out_specs=out_specs,
        out_shape=out_shape,
        scratch_shapes=[pltpu.VMEM((seq, DK_RET), BF16),
                        pltpu.VMEM((seq // CHUNK, DK_RET, CHUNK), BF16),
                        pltpu.VMEM((seq // CHUNK, DK_RET, DV_RET), BF16),
                        pltpu.VMEM((seq // CHUNK, DK_RET, DV_RET), BF16)],
        compiler_params=_params(("arbitrary", "arbitrary")),
        name="retention",
    )(*args)


def _conv_silu_chunk(src_ref, w_ref, b_ref, c, nc):
    t = CHUNK
    r0 = pl.multiple_of(c * t, t)
    cur = src_ref[pl.ds(r0, t), :].astype(F32)
    p0 = pl.multiple_of(jnp.maximum(r0 - 16, 0), 16)
    n0 = pl.multiple_of(jnp.minimum(r0 + t, nc * t - 16), 16)
    prev = jnp.where(c > 0, src_ref[pl.ds(p0, 16), :].astype(F32)[15:16, :], 0.0)
    nxt = jnp.where(c < nc - 1, src_ref[pl.ds(n0, 16), :].astype(F32)[0:1, :], 0.0)
    row = lax.broadcasted_iota(jnp.int32, (t, 1), 0)
    xm1 = jnp.where(row == 0, prev, pltpu.roll(cur, 1, 0))
    xp1 = jnp.where(row == t - 1, nxt, pltpu.roll(cur, t - 1, 0))
    return _silu(xm1 * w_ref[0:1, :] + cur * w_ref[1:2, :] + xp1 * w_ref[2:3, :] + b_ref[...])


def _exact_cumsum(tri_b, x):
    hi = x.astype(BF16)
    r1 = x - hi.astype(F32)
    mid = r1.astype(BF16)
    lo = (r1 - mid.astype(F32)).astype(BF16)
    return (jnp.dot(tri_b, hi, preferred_element_type=F32)
            + jnp.dot(tri_b, mid, preferred_element_type=F32)
            + jnp.dot(tri_b, lo, preferred_element_type=F32))


def _ssd_kernel(*refs, seq, has_init, want_state):
    it = iter(refs)
    (dsk_ref, xs_ref, z_ref, b_ref, c_ref, dt_ref, bias_ref, alog_ref,
     wx_ref, bx_ref, wb_ref, bb_ref, wc_ref, bc_ref) = (next(it) for _ in range(14))
    init_ref = st_ref = None
    if has_init:
        init_ref = next(it)
    u_ref = next(it)
    if want_state:
        st_ref = next(it)
    xc_scr, bt_scr, cc_scr, col_scr, row_scr, sf_scr, sb_scr = (next(it) for _ in range(7))

    t = CHUNK
    nc = seq // t
    quad = pl.program_id(1)
    unroll = min(nc, SSD_UNROLL)

    ii = lax.broadcasted_iota(jnp.int32, (t, t), 0)
    jj = lax.broadcasted_iota(jnp.int32, (t, t), 1)
    lower = ii >= jj
    strict_lower = ii > jj
    diag = ii == jj
    tril_b = jnp.where(lower, 1.0, 0.0).astype(BF16)
    lane = lax.broadcasted_iota(jnp.int32, (1, DT_LANES), 1)
    a_neg = -jnp.exp(alog_ref[...])
    dt_bias = bias_ref[...]

    def rows(c):
        return pl.ds(pl.multiple_of(c * t, t), t)

    def prep(c, carry):
        r = rows(c)
        xc_scr[r, :] = _conv_silu_chunk(xs_ref, wx_ref, bx_ref, c, nc).astype(BF16)
        bt_scr[c] = _conv_silu_chunk(b_ref, wb_ref, bb_ref, c, nc).T.astype(BF16)
        cc_scr[r, :] = _conv_silu_chunk(c_ref, wc_ref, bc_ref, c, nc).astype(BF16)
        dt = _softplus(dt_ref[r, :] + dt_bias)
        da = dt * a_neg
        cum = _exact_cumsum(tril_b, da)
        total = cum[t - 1:t, :]
        decay = jnp.where(lane < HQ, cum, total - cum + da)
        packed = jnp.where(lane < 2 * HQ, decay, dt)
        col_scr[r, :] = packed
        row_scr[c] = packed.T[0:4 * HQ, :]
        return carry

    lax.fori_loop(0, nc, prep, 0, unroll=unroll)

    even = lax.broadcasted_iota(jnp.int32, (1, 2 * P_SSD), 1) < P_SSD

    def pair_cols(k):
        return slice(k * 2 * P_SSD, (k + 1) * 2 * P_SSD)

    def chunk_states(c, first, tot_lane):
        rowf = row_scr[c]
        cum_r = rowf[first:first + HQ, :]
        dt_r = rowf[2 * HQ + first:3 * HQ + first, :]
        tot = cum_r[:, tot_lane:tot_lane + 1]
        w = jnp.exp(tot - cum_r) * dt_r
        dec = jnp.exp(tot)
        btf = bt_scr[c].astype(F32)
        r = rows(c)
        decs, css = [], []
        for k in range(HP):
            x_pair = xc_scr[r, pair_cols(k)]
            cs = [jnp.dot((btf * w[hh:hh + 1, :]).astype(BF16), x_pair, preferred_element_type=F32)
                  for hh in (2 * k, 2 * k + 1)]
            css.append(jnp.where(even, cs[0], cs[1]))
            decs.append(jnp.where(even, dec[2 * k:2 * k + 1, :], dec[2 * k + 1:2 * k + 2, :]))
        return decs, css

    def states(i, carry):
        s_f, s_b = carry
        c_f, c_b = i, nc - 1 - i
        dec_f, cs_f = chunk_states(c_f, 0, t - 1)
        dec_b, cs_b = chunk_states(c_b, HQ, 0)
        new_f, new_b = [], []
        for k in range(HP):
            sf_scr[c_f * HP + k] = s_f[k].astype(BF16)
            sb_scr[c_b * HP + k] = s_b[k].astype(BF16)
            new_f.append(dec_f[k] * s_f[k] + cs_f[k])
            new_b.append(dec_b[k] * s_b[k] + cs_b[k])
        return tuple(new_f), tuple(new_b)

    def head_lhs(colf, rowf, cb, cf, hh):
        fl, bl = hh, HQ + hh
        rc_f = jnp.broadcast_to(colf[:, fl:fl + 1], (t, t))
        rc_b = jnp.broadcast_to(colf[:, bl:bl + 1], (t, t))
        seg = jnp.where(lower, rc_f - rowf[fl:fl + 1, :], rc_b - rowf[bl:bl + 1, :])
        dt_f = rowf[2 * HQ + fl:2 * HQ + fl + 1, :]
        dt_b = rowf[2 * HQ + bl:2 * HQ + bl + 1, :]
        dt_sel = jnp.where(strict_lower, dt_f, jnp.where(diag, dt_f + dt_b, dt_b))
        return jnp.concatenate([(cb * (jnp.exp(seg) * dt_sel)).astype(BF16),
                                (cf * jnp.exp(rc_f)).astype(BF16),
                                (cf * jnp.exp(rc_b)).astype(BF16)], axis=1)

    def outs(c, carry):
        r = rows(c)
        colf = col_scr[r, :]
        rowf = row_scr[c]
        cmat = cc_scr[r, :]
        cf = cmat.astype(F32)
        cb = jnp.dot(cmat, bt_scr[c], preferred_element_type=F32)
        for k in range(HP):
            x_pair = xc_scr[r, pair_cols(k)]
            rhs = jnp.concatenate([x_pair, sf_scr[c * HP + k], sb_scr[c * HP + k]], axis=0)
            ys = [jnp.dot(head_lhs(colf, rowf, cb, cf, hh), rhs, preferred_element_type=F32)
                  for hh in (2 * k, 2 * k + 1)]
            h0 = quad * HQ + 2 * k
            skip = jnp.where(even, dsk_ref[h0], dsk_ref[h0 + 1])
            y = jnp.where(even, ys[0], ys[1]) + skip * x_pair.astype(F32)
            u_ref[r, pair_cols(k)] = (y * _silu(z_ref[r, pair_cols(k)].astype(F32))).astype(u_ref.dtype)
        return carry

    def pair_state(ref, d, k):
        return ref[d, 2 * k:2 * k + 2].reshape(2 * P_SSD, N_SSD).T

    if has_init:
        s0_f = tuple(pair_state(init_ref, 0, k) for k in range(HP))
        s0_b = tuple(pair_state(init_ref, 1, k) for k in range(HP))
    else:
        s0_f = s0_b = tuple(jnp.zeros((N_SSD, 2 * P_SSD), F32) for _ in range(HP))
    s_f, s_b = lax.fori_loop(0, nc, states, (s0_f, s0_b), unroll=unroll)
    lax.fori_loop(0, nc, outs, 0, unroll=unroll)
    if want_state:
        for k in range(HP):
            st_ref[0, 2 * k:2 * k + 2] = s_f[k].T.reshape(2, P_SSD, N_SSD)
            st_ref[1, 2 * k:2 * k + 2] = s_b[k].T.reshape(2, P_SSD, N_SSD)


def _ssd(proj3, dt3, d_skip, dt_bias_q, a_log_q, conv_w, conv_b, init, layer, want_state):
    b, seq, _ = proj3.shape
    wq = HQ * P_SSD
    col = lambda width, off, fn: pl.BlockSpec((None, seq, width),
                                              lambda i, q, o=off // width: (i, 0, o + fn(q)))
    ident = lambda q: q
    group = lambda q: q // (NQ // G_SSD)
    xoff = 0
    boff = D_SSD
    coff = D_SSD + G_SSD * N_SSD
    cw = lambda width, off, fn: pl.BlockSpec((D_CONV, width), lambda i, q, o=off // width: (0, o + fn(q)))
    cbias = lambda width, off, fn: pl.BlockSpec((1, width), lambda i, q, o=off // width: (0, o + fn(q)))
    in_specs = [
        pl.BlockSpec(memory_space=pltpu.SMEM),
        col(wq, OFF_XS, ident), col(wq, OFF_Z, ident), col(N_SSD, OFF_B, group), col(N_SSD, OFF_C, group),
        pl.BlockSpec((None, seq, DT_LANES), lambda i, q: (i, 0, q)),
        pl.BlockSpec((None, 1, DT_LANES), lambda i, q: (q, 0, 0)),
        pl.BlockSpec((None, 1, DT_LANES), lambda i, q: (q, 0, 0)),
        cw(wq, xoff, ident), cbias(wq, xoff, ident),
        cw(N_SSD, boff, group), cbias(N_SSD, boff, group),
        cw(N_SSD, coff, group), cbias(N_SSD, coff, group),
    ]
    args = [d_skip, proj3, proj3, proj3, proj3, dt3, dt_bias_q, a_log_q,
            conv_w, conv_b, conv_w, conv_b, conv_w, conv_b]
    st_spec = pl.BlockSpec((None, 2, HQ, P_SSD, N_SSD), lambda i, q: (i, 0, q, 0, 0))
    if init is not None:
        in_specs.append(pl.BlockSpec((None, None, 2, HQ, P_SSD, N_SSD),
                                     lambda i, q: (i, layer, 0, q, 0, 0)))
        args.append(init)
    out_specs = [pl.BlockSpec((None, seq, wq), lambda i, q: (i, 0, q))]
    out_shape = [jax.ShapeDtypeStruct((b, seq, D_SSD), BF16)]
    if want_state:
        out_specs.append(st_spec)
        out_shape.append(jax.ShapeDtypeStruct((b, 2, H_SSD, P_SSD, N_SSD), F32))
    nc = seq // CHUNK
    return pl.pallas_call(
        functools.partial(_ssd_kernel, seq=seq, has_init=init is not None, want_state=want_state),
        grid=(b, NQ),
        in_specs=in_specs,
        out_specs=out_specs,
        out_shape=out_shape,
        scratch_shapes=[
            pltpu.VMEM((seq, wq), BF16), pltpu.VMEM((nc, N_SSD, CHUNK), BF16), pltpu.VMEM((seq, N_SSD), BF16),
            pltpu.VMEM((seq, DT_LANES), F32), pltpu.VMEM((nc, 4 * HQ, CHUNK), F32),
            pltpu.VMEM((nc * HP, N_SSD, 2 * P_SSD), BF16), pltpu.VMEM((nc * HP, N_SSD, 2 * P_SSD), BF16),
        ],
        compiler_params=_params(("arbitrary", "arbitrary")),
        name="ssd",
    )(*args)


def _out_kernel(yr_ref, us_ref, nw_ref, x_ref, g_ref, w_ref, o_ref, a_scr):
    @pl.when(pl.program_id(1) == 0)
    def _():
        a_scr[:, :D_RET] = yr_ref[...]
        a_scr[:, D_RET:] = _rms(us_ref[...].astype(F32), nw_ref[...]).astype(BF16)

    acc = jnp.dot(a_scr[...], w_ref[...], preferred_element_type=F32)
    o_ref[...] = x_ref[...] + g_ref[...] * acc


def _out_proj(yr, us, ssd_nw, x2, rows_per_cond, gate, w_out_all, layer):
    m = x2.shape[0]
    tm, tn = 1024, 1024
    return pl.pallas_call(
        _out_kernel,
        grid=(m // tm, D_MODEL // tn),
        in_specs=[
            pl.BlockSpec((tm, D_RET), lambda i, j: (i, 0)),
            pl.BlockSpec((tm, D_SSD), lambda i, j: (i, 0)),
            pl.BlockSpec((1, D_SSD), lambda i, j: (0, 0)),
            pl.BlockSpec((tm, tn), lambda i, j: (i, j)),
            pl.BlockSpec((None, 1, tn), lambda i, j: ((i * tm) // rows_per_cond, 0, j)),
            pl.BlockSpec((None, D_RET + D_SSD, tn), lambda i, j: (layer, 0, j)),
        ],
        out_specs=pl.BlockSpec((tm, tn), lambda i, j: (i, j)),
        out_shape=jax.ShapeDtypeStruct((m, D_MODEL), F32),
        scratch_shapes=[pltpu.VMEM((tm, D_RET + D_SSD), BF16)],
        compiler_params=_params(("arbitrary", "arbitrary")),
        name="out_proj",
    )(yr, us, ssd_nw, x2, gate, w_out_all)


def _ffn_kernel(*refs, final):
    it = iter(refs)
    x_ref, nw_ref, sc_ref, sh_ref, g_ref, w1_ref, w2_ref = (next(it) for _ in range(7))
    fw_ref = next(it) if final else None
    o_ref, h_scr, acc_scr = next(it), next(it), next(it)
    f = pl.program_id(1)

    @pl.when(f == 0)
    def _():
        h = _rms(x_ref[...], nw_ref[...]) * (1.0 + sc_ref[...]) + sh_ref[...]
        h_scr[...] = h.astype(BF16)
        acc_scr[...] = jnp.zeros_like(acc_scr)

    a = jnp.maximum(jnp.dot(h_scr[...], w1_ref[...], preferred_element_type=F32), 0.0)
    acc_scr[...] += jnp.dot((a * a).astype(BF16), w2_ref[...], preferred_element_type=F32)

    @pl.when(f == pl.num_programs(1) - 1)
    def _():
        y = x_ref[...] + g_ref[...] * acc_scr[...]
        if final:
            y = _rms(y, fw_ref[...])
        o_ref[...] = y


def _ffn(x2, rows_per_cond, nw, sc, sh, gate, w1, w2, layer, final_w):
    m = x2.shape[0]
    tm, tf = 512, 1024
    cond_map = lambda i, f: ((i * tm) // rows_per_cond, 0, 0)
    vec = pl.BlockSpec((1, D_MODEL), lambda i, f: (0, 0))
    cvec = pl.BlockSpec((None, 1, D_MODEL), cond_map)
    in_specs = [pl.BlockSpec((tm, D_MODEL), lambda i, f: (i, 0)), vec, cvec, cvec, cvec,
                pl.BlockSpec((None, D_MODEL, tf), lambda i, f: (layer, 0, f)),
                pl.BlockSpec((None, tf, D_MODEL), lambda i, f: (layer, f, 0))]
    args = [x2, nw, sc, sh, gate, w1, w2]
    if final_w is not None:
        in_specs.append(vec)
        args.append(final_w)
    return pl.pallas_call(
        functools.partial(_ffn_kernel, final=final_w is not None),
        grid=(m // tm, D_FF // tf),
        in_specs=in_specs,
        out_specs=pl.BlockSpec((tm, D_MODEL), lambda i, f: (i, 0)),
        out_shape=jax.ShapeDtypeStruct((m, D_MODEL), F32),
        scratch_shapes=[pltpu.VMEM((tm, D_MODEL), BF16), pltpu.VMEM((tm, D_MODEL), F32)],
        compiler_params=_params(("arbitrary", "arbitrary")),
        name="ffn",
    )(*args)


def _rope_tables(seq):
    pos = jnp.arange(seq)
    row = (pos // GRID_W).astype(F32)
    col = (pos % GRID_W).astype(F32)
    half = DK_RET // 2
    inv = 1.0 / (ROPE_BASE ** (jnp.arange(0, half, 2, dtype=F32) / half))
    ang = jnp.concatenate([row[:, None] * inv, col[:, None] * inv], -1)
    cs, sn = jnp.cos(ang), jnp.sin(ang)
    return jnp.concatenate([cs, cs], -1), jnp.concatenate([-sn, sn], -1)


def _quad_lanes(v):
    pad = jnp.zeros((DT_LANES - 4 * HQ,), v.dtype)
    quads = []
    for q in range(NQ):
        fb = [v[0, q * HQ:(q + 1) * HQ], v[1, q * HQ:(q + 1) * HQ]]
        quads.append(jnp.concatenate(fb + fb + [pad]))
    return jnp.stack(quads)[:, None, :]


def _dt_weight(w_dt):
    parts = []
    for q in range(NQ):
        fb = [w_dt[:, q * HQ:(q + 1) * HQ], w_dt[:, H_SSD + q * HQ:H_SSD + (q + 1) * HQ]]
        parts += fb + fb
    parts.append(jnp.zeros((D_MODEL, DT_LANES - NQ * QUAD_LANES), w_dt.dtype))
    return jnp.concatenate(parts, axis=1)


def kernel(x_prompt, x_sample, state_ret, state_ssd, c, c_ctx, w_ada, b_ada, norm1_w, w_in,
           ret_log_decay, conv_w, conv_b, dt_bias, a_log, d_skip, ssd_norm_w, w_out, norm2_w,
           w_ff1, w_ff2, final_norm_w):
    bp, sp, _ = x_prompt.shape
    bs, ss, _ = x_sample.shape

    cond = jnp.zeros((COND_ROWS, D_MODEL), F32).at[:bs].set(c).at[bs].set(c_ctx)
    mod = _ada(cond, w_ada, b_ada).reshape(DEPTH, COND_ROWS, N_MOD, D_MODEL)

    rope = _rope_tables(ss)
    final_w = final_norm_w.reshape(1, D_MODEL)
    w_in_b = w_in.astype(BF16)
    w_dt_b = [_dt_weight(w_in[l, :, D_MAIN:]).astype(BF16) for l in range(DEPTH)]
    w_out_b = w_out.astype(BF16)
    w_ff1_b = w_ff1.astype(BF16)
    w_ff2_b = w_ff2.astype(BF16)

    def run_group(x, mod_rows, rope_tabs, states, want_state):
        b, seq, _ = x.shape
        x2 = x.reshape(b * seq, D_MODEL)
        rows_per_cond = seq if mod_rows.stop - mod_rows.start > 1 else b * seq
        ret_states, ssd_states = [], []
        for l in range(DEPTH):
            mv = [mod[l, mod_rows, k][:, None, :] for k in range(N_MOD)]
            sh1, sc1, g1, sh2, sc2, g2 = mv
            proj, dt = _in_proj(x2, rows_per_cond, norm1_w[l].reshape(1, D_MODEL), sc1, sh1,
                                w_in_b, l, w_dt_b[l])
            proj3 = proj.reshape(b, seq, D_MAIN)
            dt3 = dt.reshape(b, seq, NQ * DT_LANES)
            init_r, init_s = (None, None) if states is None else states
            ret = _retention(proj3, ret_log_decay[l], rope_tabs, init_r, l, want_state)
            ssd = _ssd(proj3, dt3, d_skip[l], _quad_lanes(dt_bias[l]), _quad_lanes(a_log[l]),
                       conv_w[l], conv_b[l].reshape(1, D_XBC), init_s, l, want_state)
            if want_state:
                ret_states.append(ret[1])
                ssd_states.append(ssd[1])
            x2 = _out_proj(ret[0].reshape(b * seq, D_RET), ssd[0].reshape(b * seq, D_SSD),
                           ssd_norm_w[l].reshape(1, D_SSD), x2, rows_per_cond, g1, w_out_b, l)
            x2 = _ffn(x2, rows_per_cond, norm2_w[l].reshape(1, D_MODEL), sc2, sh2, g2,
                      w_ff1_b, w_ff2_b, l, final_w if l == DEPTH - 1 else None)
        return x2.reshape(b, seq, D_MODEL), ret_states, ssd_states

    y_prompt, ret_states, ssd_states = run_group(x_prompt, slice(bs, bs + 1), None, None, True)
    y_sample, _, _ = run_group(x_sample, slice(0, bs), rope, (state_ret, state_ssd), False)
    return (y_prompt, y_sample, jnp.stack(ret_states, 1), jnp.stack(ssd_states, 1))
```

```python
import functools

import jax
import jax.numpy as jnp
from jax import lax
from jax.experimental import pallas as pl
from jax.experimental.pallas import tpu as pltpu

F32 = jnp.float32
BF16 = jnp.bfloat16

D_MODEL = 2048
DEPTH = 4
CHUNK = 128
H_RET = 8
DK_RET = 128
DV_RET = 128
D_RET = H_RET * DV_RET
H_SSD = 16
P_SSD = 64
D_SSD = H_SSD * P_SSD
G_SSD = 2
N_SSD = 128
D_CONV = 3
D_XBC = D_SSD + 2 * G_SSD * N_SSD
D_FF = 4 * D_MODEL
N_MOD = 6
EPS = 1e-6
ROPE_BASE = 10000.0
GRID_W = 64

D_MAIN = 4 * D_RET + D_SSD + D_XBC
OFF_Q, OFF_K, OFF_V, OFF_G = 0, D_RET, 2 * D_RET, 3 * D_RET
OFF_Z = 4 * D_RET
OFF_XS = OFF_Z + D_SSD
OFF_B = OFF_XS + D_SSD
OFF_C = OFF_B + G_SSD * N_SSD

HQ = 4
NQ = H_SSD // HQ
HP = HQ // 2
DT_LANES = 128
QUAD_LANES = 4 * HQ
COND_ROWS = 16
SCAN_UNROLL = 8
SSD_UNROLL = 2
V7X_VMEM_LIMIT = 56 * 1024 * 1024


def _silu(x):
    return x * jax.nn.sigmoid(x)


def _softplus(x):
    return jnp.maximum(x, 0.0) + jnp.log1p(jnp.exp(-jnp.abs(x)))


def _rms(x, w):
    return x * lax.rsqrt(jnp.mean(x * x, axis=-1, keepdims=True) + EPS) * w


def _params(sem):
    return pltpu.CompilerParams(dimension_semantics=sem, vmem_limit_bytes=V7X_VMEM_LIMIT)


def _ada_kernel(cond_ref, w_ref, b_ref, o_ref):
    s = _silu(cond_ref[...]).astype(BF16)
    o_ref[...] = jnp.dot(s, w_ref[...].astype(BF16), preferred_element_type=F32) + b_ref[...]


def _ada(cond, w_ada, b_ada):
    tn = 1024
    n = N_MOD * D_MODEL
    return pl.pallas_call(
        _ada_kernel,
        grid=(DEPTH, n // tn),
        in_specs=[
            pl.BlockSpec((COND_ROWS, D_MODEL), lambda l, j: (0, 0)),
            pl.BlockSpec((None, D_MODEL, tn), lambda l, j: (l, 0, j)),
            pl.BlockSpec((None, 1, tn), lambda l, j: (l, 0, j)),
        ],
        out_specs=pl.BlockSpec((None, COND_ROWS, tn), lambda l, j: (l, 0, j)),
        out_shape=jax.ShapeDtypeStruct((DEPTH, COND_ROWS, n), F32),
        compiler_params=_params(("arbitrary", "arbitrary")),
        name="ada_mod",
    )(cond, w_ada, b_ada.reshape(DEPTH, 1, n))


def _modulated_norm(x, nw, sc, sh):
    return (_rms(x, nw) * (1.0 + sc) + sh).astype(BF16)


def _in_kernel(x_ref, nw_ref, sc_ref, sh_ref, w_ref, wdt_ref, o_ref, dt_ref, h_cur, h_next, *, nj):
    i, j = pl.program_id(0), pl.program_id(1)
    part = x_ref.shape[0] // nj
    rows = pl.ds(pl.multiple_of(j * part, part), part)

    def stage():
        h_next[rows, :] = _modulated_norm(x_ref[rows, :], nw_ref[...], sc_ref[...], sh_ref[...])

    @pl.when(i == 0)
    def _():
        stage()

    @pl.when(i > 0)
    def _():
        stage()
        dt = jnp.dot(h_cur[rows, :], wdt_ref[...], preferred_element_type=F32)
        for q in range(NQ):
            shift = (DT_LANES - QUAD_LANES * q) % DT_LANES
            dt_ref[rows, q * DT_LANES:(q + 1) * DT_LANES] = pltpu.roll(dt, shift, 1) if shift else dt
        o_ref[...] = jnp.dot(h_cur[...], w_ref[...], preferred_element_type=F32).astype(o_ref.dtype)

    @pl.when(j == nj - 1)
    def _():
        h_cur[...] = h_next[...]


def _in_proj(x2, rows_per_cond, nw, sc, sh, w_in_all, layer, w_dt):
    m = x2.shape[0]
    tm, tn = 512, D_MAIN // 2
    nt = m // tm
    staged = lambda i: jnp.minimum(i, nt - 1)
    done = lambda i: jnp.maximum(i - 1, 0)
    cond_map = lambda i, j: ((staged(i) * tm) // rows_per_cond, 0, 0)
    return pl.pallas_call(
        functools.partial(_in_kernel, nj=D_MAIN // tn),
        grid=(nt + 1, D_MAIN // tn),
        in_specs=[
            pl.BlockSpec((tm, D_MODEL), lambda i, j: (staged(i), 0)),
            pl.BlockSpec((1, D_MODEL), lambda i, j: (0, 0)),
            pl.BlockSpec((None, 1, D_MODEL), cond_map),
            pl.BlockSpec((None, 1, D_MODEL), cond_map),
            pl.BlockSpec((None, D_MODEL, tn), lambda i, j: (layer, 0, j)),
            pl.BlockSpec((D_MODEL, DT_LANES), lambda i, j: (0, 0)),
        ],
        out_specs=[
            pl.BlockSpec((tm, tn), lambda i, j: (done(i), jnp.where(i > 0, j, 0))),
            pl.BlockSpec((tm, NQ * DT_LANES), lambda i, j: (done(i), 0)),
        ],
        out_shape=[
            jax.ShapeDtypeStruct((m, D_MAIN), BF16),
            jax.ShapeDtypeStruct((m, NQ * DT_LANES), F32),
        ],
        scratch_shapes=[pltpu.VMEM((tm, D_MODEL), BF16), pltpu.VMEM((tm, D_MODEL), BF16)],
        compiler_params=_params(("arbitrary", "arbitrary")),
        name="in_proj",
    )(x2, nw, sc, sh, w_in_all, w_dt)


def _ret_kernel(*refs, seq, n_seq, rope, has_init, want_state):
    it = iter(refs)
    ld_ref, q_ref, k_ref, v_ref, g_ref = (next(it) for _ in range(5))
    cos_ref = sin_ref = init_ref = st_ref = None
    if rope:
        cos_ref, sin_ref = next(it), next(it)
    if has_init:
        init_ref = next(it)
    y_ref = next(it)
    if want_state:
        st_ref = next(it)
    q_scr, kt_scr, sf_scr, sb_scr = next(it), next(it), next(it), next(it)

    t = CHUNK
    nc = seq // t
    head = pl.program_id(1)
    la_f = ld_ref[0, head]
    la_b = ld_ref[1, head]

    ii = lax.broadcasted_iota(jnp.int32, (t, t), 0)
    jj = lax.broadcasted_iota(jnp.int32, (t, t), 1)
    dist = (ii - jj).astype(F32)
    lower = ii >= jj
    upper = ii <= jj
    dmat = (jnp.where(lower, jnp.exp(la_f * jnp.where(lower, dist, 0.0)), 0.0)
            + jnp.where(upper, jnp.exp(la_b * jnp.where(upper, -dist, 0.0)), 0.0))
    pos = lax.broadcasted_iota(jnp.int32, (t, 1), 0).astype(F32)
    lane = lax.broadcasted_iota(jnp.int32, (1, t), 1).astype(F32)
    e_f = jnp.exp(la_f * (pos + 1.0))
    e_b = jnp.exp(la_b * (t - pos))
    w_f = jnp.exp(la_f * (t - 1.0 - lane))
    w_b = jnp.exp(la_b * lane)
    dec_f = jnp.exp(jnp.full((1, 1), t, F32) * la_f)
    dec_b = jnp.exp(jnp.full((1, 1), t, F32) * la_b)
    scale = DK_RET ** -0.5
    unroll = min(nc, SCAN_UNROLL)

    def rows(c):
        return pl.ds(pl.multiple_of(c * t, t), t)

    def prep(c, carry):
        r = rows(c)
        q = q_ref[r, :].astype(F32)
        k = k_ref[r, :].astype(F32)
        if rope:
            cs, sn = cos_ref[r, :], sin_ref[r, :]
            q = q * cs + pltpu.roll(q, DK_RET // 2, 1) * sn
            k = k * cs + pltpu.roll(k, DK_RET // 2, 1) * sn
        q_scr[r, :] = (q * scale).astype(BF16)
        kt_scr[c] = k.T.astype(BF16)
        return carry

    lax.fori_loop(0, nc, prep, 0, unroll=unroll)

    def chunk_state(c, w_row):
        kw = (kt_scr[c].astype(F32) * w_row).astype(BF16)
        return jnp.dot(kw, v_ref[rows(c), :], preferred_element_type=F32)

    def states(i, carry, base=0, last=nc - 1):
        s_f, s_b = carry
        c_f, c_b = base + i, base + last - i
        sf_scr[c_f] = s_f.astype(BF16)
        sb_scr[c_b] = s_b.astype(BF16)
        return dec_f * s_f + chunk_state(c_f, w_f), dec_b * s_b + chunk_state(c_b, w_b)

    def outs(c, carry):
        r = rows(c)
        q = q_scr[r, :]
        att = jnp.dot(q, kt_scr[c], preferred_element_type=F32)
        qf = q.astype(F32)
        a = jnp.concatenate([(att * dmat).astype(BF16), (qf * e_f).astype(BF16),
                             (qf * e_b).astype(BF16)], axis=1)
        rhs = jnp.concatenate([v_ref[r, :], sf_scr[c], sb_scr[c]], axis=0)
        y = jnp.dot(a, rhs, preferred_element_type=F32)
        yn = y * lax.rsqrt(jnp.mean(y * y, axis=-1, keepdims=True) + EPS)
        y_ref[r, :] = (_silu(g_ref[r, :].astype(F32)) * yn).astype(y_ref.dtype)
        return carry

    zero = jnp.zeros((DK_RET, DV_RET), F32)
    if n_seq == 1:
        s0_f, s0_b = (init_ref[0].T, init_ref[1].T) if has_init else (zero, zero)
        s_f, s_b = lax.fori_loop(0, nc, states, (s0_f, s0_b), unroll=unroll)
        if want_state:
            st_ref[0, 0] = s_f.T
            st_ref[0, 1] = s_b.T
    else:
        assert not has_init and not rope
        per = nc // n_seq

        def sequence(g, carry):
            s_f = s_b = zero
            for u in range(per):
                s_f, s_b = states(u, (s_f, s_b), base=g * per, last=per - 1)
            if want_state:
                st_ref[g, 0] = s_f.T
                st_ref[g, 1] = s_b.T
            return carry

        lax.fori_loop(0, n_seq, sequence, 0, unroll=min(n_seq, max(1, SCAN_UNROLL // per)))
    lax.fori_loop(0, nc, outs, 0, unroll=unroll)


def _retention(proj3, ret_ld, rope_tabs, init, layer, want_state, n_seq=1):
    b, seq, _ = proj3.shape
    hblk = lambda off: pl.BlockSpec((None, seq, DK_RET), lambda i, h, o=off // DK_RET: (i, 0, o + h))
    in_specs = [pl.BlockSpec(memory_space=pltpu.SMEM), hblk(OFF_Q), hblk(OFF_K), hblk(OFF_V), hblk(OFF_G)]
    args = [ret_ld, proj3, proj3, proj3, proj3]
    if rope_tabs is not None:
        in_specs += [pl.BlockSpec((seq, DK_RET), lambda i, h: (0, 0))] * 2
        args += list(rope_tabs)
    st_spec = pl.BlockSpec((n_seq, 2, None, DV_RET, DK_RET), lambda i, h: (i, 0, h, 0, 0))
    if init is not None:
        in_specs.append(pl.BlockSpec((None, None, 2, None, DV_RET, DK_RET),
                                     lambda i, h: (i, layer, 0, h, 0, 0)))
        args.append(init)
    out_specs = [pl.BlockSpec((None, seq, DV_RET), lambda i, h: (i, 0, h))]
    out_shape = [jax.ShapeDtypeStruct((b, seq, D_RET), BF16)]
    if want_state:
        out_specs.append(st_spec)
        out_shape.append(jax.ShapeDtypeStruct((b * n_seq, 2, H_RET, DV_RET, DK_RET), F32))
    return pl.pallas_call(
        functools.partial(_ret_kernel, seq=seq, n_seq=n_seq, rope=rope_tabs is not None,
                          has_init=init is not None, want_state=want_state),
        grid=(b, H_RET),
        in_specs=in_specs,
        out_specs=out_specs,
        out_shape=out_shape,
        scratch_shapes=[pltpu.VMEM((seq, DK_RET), BF16),
                        pltpu.VMEM((seq // CHUNK, DK_RET, CHUNK), BF16),
                        pltpu.VMEM((seq // CHUNK, DK_RET, DV_RET), BF16),
                        pltpu.VMEM((seq // CHUNK, DK_RET, DV_RET), BF16)],
        compiler_params=_params(("arbitrary", "arbitrary")),
        name="retention",
    )(*args)


def _conv_silu_chunk(src_ref, w_ref, b_ref, c, nc):
    t = CHUNK
    r0 = pl.multiple_of(c * t, t)
    cur = src_ref[pl.ds(r0, t), :].astype(F32)
    p0 = pl.multiple_of(jnp.maximum(r0 - 16, 0), 16)
    n0 = pl.multiple_of(jnp.minimum(r0 + t, nc * t - 16), 16)
    prev = jnp.where(c > 0, src_ref[pl.ds(p0, 16), :].astype(F32)[15:16, :], 0.0)
    nxt = jnp.where(c < nc - 1, src_ref[pl.ds(n0, 16), :].astype(F32)[0:1, :], 0.0)
    row = lax.broadcasted_iota(jnp.int32, (t, 1), 0)
    xm1 = jnp.where(row == 0, prev, pltpu.roll(cur, 1, 0))
    xp1 = jnp.where(row == t - 1, nxt, pltpu.roll(cur, t - 1, 0))
    return _silu(xm1 * w_ref[0:1, :] + cur * w_ref[1:2, :] + xp1 * w_ref[2:3, :] + b_ref[...])


def _exact_cumsum(tri_b, x):
    hi = x.astype(BF16)
    r1 = x - hi.astype(F32)
    mid = r1.astype(BF16)
    lo = (r1 - mid.astype(F32)).astype(BF16)
    return (jnp.dot(tri_b, hi, preferred_element_type=F32)
            + jnp.dot(tri_b, mid, preferred_element_type=F32)
            + jnp.dot(tri_b, lo, preferred_element_type=F32))


def _ssd_kernel(*refs, seq, has_init, want_state):
    it = iter(refs)
    (dsk_ref, xs_ref, z_ref, b_ref, c_ref, dt_ref, bias_ref, alog_ref,
     wx_ref, bx_ref, wb_ref, bb_ref, wc_ref, bc_ref) = (next(it) for _ in range(14))
    init_ref = st_ref = None
    if has_init:
        init_ref = next(it)
    u_ref = next(it)
    if want_state:
        st_ref = next(it)
    xc_scr, bt_scr, cc_scr, col_scr, row_scr, sf_scr, sb_scr = (next(it) for _ in range(7))

    t = CHUNK
    nc = seq // t
    quad = pl.program_id(1)
    unroll = min(nc, SSD_UNROLL)

    ii = lax.broadcasted_iota(jnp.int32, (t, t), 0)
    jj = lax.broadcasted_iota(jnp.int32, (t, t), 1)
    lower = ii >= jj
    strict_lower = ii > jj
    diag = ii == jj
    tril_b = jnp.where(lower, 1.0, 0.0).astype(BF16)
    lane = lax.broadcasted_iota(jnp.int32, (1, DT_LANES), 1)
    a_neg = -jnp.exp(alog_ref[...])
    dt_bias = bias_ref[...]

    def rows(c):
        return pl.ds(pl.multiple_of(c * t, t), t)

    def prep(c, carry):
        r = rows(c)
        xc_scr[r, :] = _conv_silu_chunk(xs_ref, wx_ref, bx_ref, c, nc).astype(BF16)
        bt_scr[c] = _conv_silu_chunk(b_ref, wb_ref, bb_ref, c, nc).T.astype(BF16)
        cc_scr[r, :] = _conv_silu_chunk(c_ref, wc_ref, bc_ref, c, nc).astype(BF16)
        dt = _softplus(dt_ref[r, :] + dt_bias)
        da = dt * a_neg
        cum = _exact_cumsum(tril_b, da)
        total = cum[t - 1:t, :]
        decay = jnp.where(lane < HQ, cum, total - cum + da)
        packed = jnp.where(lane < 2 * HQ, decay, dt)
        col_scr[r, :] = packed
        row_scr[c] = packed.T[0:4 * HQ, :]
        return carry

    lax.fori_loop(0, nc, prep, 0, unroll=unroll)

    even = lax.broadcasted_iota(jnp.int32, (1, 2 * P_SSD), 1) < P_SSD

    def pair_cols(k):
        return slice(k * 2 * P_SSD, (k + 1) * 2 * P_SSD)

    def chunk_states(c, first, tot_lane):
        rowf = row_scr[c]
        cum_r = rowf[first:first + HQ, :]
        dt_r = rowf[2 * HQ + first:3 * HQ + first, :]
        tot = cum_r[:, tot_lane:tot_lane + 1]
        w = jnp.exp(tot - cum_r) * dt_r
        dec = jnp.exp(tot)
        btf = bt_scr[c].astype(F32)
        r = rows(c)
        decs, css = [], []
        for k in range(HP):
            x_pair = xc_scr[r, pair_cols(k)]
            cs = [jnp.dot((btf * w[hh:hh + 1, :]).astype(BF16), x_pair, preferred_element_type=F32)
                  for hh in (2 * k, 2 * k + 1)]
            css.append(jnp.where(even, cs[0], cs[1]))
            decs.append(jnp.where(even, dec[2 * k:2 * k + 1, :], dec[2 * k + 1:2 * k + 2, :]))
        return decs, css

    def states(i, carry):
        s_f, s_b = carry
        c_f, c_b = i, nc - 1 - i
        dec_f, cs_f = chunk_states(c_f, 0, t - 1)
        dec_b, cs_b = chunk_states(c_b, HQ, 0)
        new_f, new_b = [], []
        for k in range(HP):
            sf_scr[c_f * HP + k] = s_f[k].astype(BF16)
            sb_scr[c_b * HP + k] = s_b[k].astype(BF16)
            new_f.append(dec_f[k] * s_f[k] + cs_f[k])
            new_b.append(dec_b[k] * s_b[k] + cs_b[k])
        return tuple(new_f), tuple(new_b)

    def head_lhs(colf, rowf, cb, cf, hh):
        fl, bl = hh, HQ + hh
        rc_f = jnp.broadcast_to(colf[:, fl:fl + 1], (t, t))
        rc_b = jnp.broadcast_to(colf[:, bl:bl + 1], (t, t))
        seg = jnp.where(lower, rc_f - rowf[fl:fl + 1, :], rc_b - rowf[bl:bl + 1, :])
        dt_f = rowf[2 * HQ + fl:2 * HQ + fl + 1, :]
        dt_b = rowf[2 * HQ + bl:2 * HQ + bl + 1, :]
        dt_sel = jnp.where(strict_lower, dt_f, jnp.where(diag, dt_f + dt_b, dt_b))
        return jnp.concatenate([(cb * (jnp.exp(seg) * dt_sel)).astype(BF16),
                                (cf * jnp.exp(rc_f)).astype(BF16),
                                (cf * jnp.exp(rc_b)).astype(BF16)], axis=1)

    def outs(c, carry):
        r = rows(c)
        colf = col_scr[r, :]
        rowf = row_scr[c]
        cmat = cc_scr[r, :]
        cf = cmat.astype(F32)
        cb = jnp.dot(cmat, bt_scr[c], preferred_element_type=F32)
        for k in range(HP):
            x_pair = xc_scr[r, pair_cols(k)]
            rhs = jnp.concatenate([x_pair, sf_scr[c * HP + k], sb_scr[c * HP + k]], axis=0)
            ys = [jnp.dot(head_lhs(colf, rowf, cb, cf, hh), rhs, preferred_element_type=F32)
                  for hh in (2 * k, 2 * k + 1)]
            h0 = quad * HQ + 2 * k
            skip = jnp.where(even, dsk_ref[h0], dsk_ref[h0 + 1])
            y = jnp.where(even, ys[0], ys[1]) + skip * x_pair.astype(F32)
            u_ref[r, pair_cols(k)] = (y * _silu(z_ref[r, pair_cols(k)].astype(F32))).astype(u_ref.dtype)
        return carry

    def pair_state(ref, d, k):
        return ref[d, 2 * k:2 * k + 2].reshape(2 * P_SSD, N_SSD).T

    if has_init:
        s0_f = tuple(pair_state(init_ref, 0, k) for k in range(HP))
        s0_b = tuple(pair_state(init_ref, 1, k) for k in range(HP))
    else:
        s0_f = s0_b = tuple(jnp.zeros((N_SSD, 2 * P_SSD), F32) for _ in range(HP))
    s_f, s_b = lax.fori_loop(0, nc, states, (s0_f, s0_b), unroll=min(nc, 2 * SSD_UNROLL))
    lax.fori_loop(0, nc, outs, 0, unroll=unroll)
    if want_state:
        for k in range(HP):
            st_ref[0, 2 * k:2 * k + 2] = s_f[k].T.reshape(2, P_SSD, N_SSD)
            st_ref[1, 2 * k:2 * k + 2] = s_b[k].T.reshape(2, P_SSD, N_SSD)


def _ssd(proj3, dt3, d_skip, dt_bias_q, a_log_q, conv_w, conv_b, init, layer, want_state):
    b, seq, _ = proj3.shape
    wq = HQ * P_SSD
    col = lambda width, off, fn: pl.BlockSpec((None, seq, width),
                                              lambda i, q, o=off // width: (i, 0, o + fn(q)))
    ident = lambda q: q
    group = lambda q: q // (NQ // G_SSD)
    xoff = 0
    boff = D_SSD
    coff = D_SSD + G_SSD * N_SSD
    cw = lambda width, off, fn: pl.BlockSpec((D_CONV, width), lambda i, q, o=off // width: (0, o + fn(q)))
    cbias = lambda width, off, fn: pl.BlockSpec((1, width), lambda i, q, o=off // width: (0, o + fn(q)))
    in_specs = [
        pl.BlockSpec(memory_space=pltpu.SMEM),
        col(wq, OFF_XS, ident), col(wq, OFF_Z, ident), col(N_SSD, OFF_B, group), col(N_SSD, OFF_C, group),
        pl.BlockSpec((None, seq, DT_LANES), lambda i, q: (i, 0, q)),
        pl.BlockSpec((None, 1, DT_LANES), lambda i, q: (q, 0, 0)),
        pl.BlockSpec((None, 1, DT_LANES), lambda i, q: (q, 0, 0)),
        cw(wq, xoff, ident), cbias(wq, xoff, ident),
        cw(N_SSD, boff, group), cbias(N_SSD, boff, group),
        cw(N_SSD, coff, group), cbias(N_SSD, coff, group),
    ]
    args = [d_skip, proj3, proj3, proj3, proj3, dt3, dt_bias_q, a_log_q,
            conv_w, conv_b, conv_w, conv_b, conv_w, conv_b]
    st_spec = pl.BlockSpec((None, 2, HQ, P_SSD, N_SSD), lambda i, q: (i, 0, q, 0, 0))
    if init is not None:
        in_specs.append(pl.BlockSpec((None, None, 2, HQ, P_SSD, N_SSD),
                                     lambda i, q: (i, layer, 0, q, 0, 0)))
        args.append(init)
    out_specs = [pl.BlockSpec((None, seq, wq), lambda i, q: (i, 0, q))]
    out_shape = [jax.ShapeDtypeStruct((b, seq, D_SSD), BF16)]
    if want_state:
        out_specs.append(st_spec)
        out_shape.append(jax.ShapeDtypeStruct((b, 2, H_SSD, P_SSD, N_SSD), F32))
    nc = seq // CHUNK
    return pl.pallas_call(
        functools.partial(_ssd_kernel, seq=seq, has_init=init is not None, want_state=want_state),
        grid=(b, NQ),
        in_specs=in_specs,
        out_specs=out_specs,
        out_shape=out_shape,
        scratch_shapes=[
            pltpu.VMEM((seq, wq), BF16), pltpu.VMEM((nc, N_SSD, CHUNK), BF16), pltpu.VMEM((seq, N_SSD), BF16),
            pltpu.VMEM((seq, DT_LANES), F32), pltpu.VMEM((nc, 4 * HQ, CHUNK), F32),
            pltpu.VMEM((nc * HP, N_SSD, 2 * P_SSD), BF16), pltpu.VMEM((nc * HP, N_SSD, 2 * P_SSD), BF16),
        ],
        compiler_params=_params(("arbitrary", "arbitrary")),
        name="ssd",
    )(*args)


def _out_kernel(yr_ref, us_ref, nw_ref, x_ref, g_ref, w_ref, o_ref, a_scr):
    @pl.when(pl.program_id(1) == 0)
    def _():
        a_scr[:, :D_RET] = yr_ref[...]
        a_scr[:, D_RET:] = _rms(us_ref[...].astype(F32), nw_ref[...]).astype(BF16)

    acc = jnp.dot(a_scr[...], w_ref[...], preferred_element_type=F32)
    o_ref[...] = x_ref[...] + g_ref[...] * acc


def _out_proj(yr, us, ssd_nw, x2, rows_per_cond, gate, w_out_all, layer):
    m = x2.shape[0]
    tm, tn = 1024, 1024
    return pl.pallas_call(
        _out_kernel,
        grid=(m // tm, D_MODEL // tn),
        in_specs=[
            pl.BlockSpec((tm, D_RET), lambda i, j: (i, 0)),
            pl.BlockSpec((tm, D_SSD), lambda i, j: (i, 0)),
            pl.BlockSpec((1, D_SSD), lambda i, j: (0, 0)),
            pl.BlockSpec((tm, tn), lambda i, j: (i, j)),
            pl.BlockSpec((None, 1, tn), lambda i, j: ((i * tm) // rows_per_cond, 0, j)),
            pl.BlockSpec((None, D_RET + D_SSD, tn), lambda i, j: (layer, 0, j)),
        ],
        out_specs=pl.BlockSpec((tm, tn), lambda i, j: (i, j)),
        out_shape=jax.ShapeDtypeStruct((m, D_MODEL), F32),
        scratch_shapes=[pltpu.VMEM((tm, D_RET + D_SSD), BF16)],
        compiler_params=_params(("arbitrary", "arbitrary")),
        name="out_proj",
    )(yr, us, ssd_nw, x2, gate, w_out_all)


def _ffn_kernel(*refs, final, nf):
    it = iter(refs)
    x_ref, res_ref, nw_ref, sc_ref, sh_ref, g_ref, w1_ref, w2_ref = (next(it) for _ in range(8))
    fw_ref = next(it) if final else None
    o_ref, h_cur, h_next, acc_scr = next(it), next(it), next(it), next(it)
    i, f = pl.program_id(0), pl.program_id(1)
    part = x_ref.shape[0] // nf
    rows = pl.ds(pl.multiple_of(f * part, part), part)

    def stage():
        h_next[rows, :] = _modulated_norm(x_ref[rows, :], nw_ref[...], sc_ref[...], sh_ref[...])

    def step(first):
        stage()
        a = jnp.maximum(jnp.dot(h_cur[...], w1_ref[...], preferred_element_type=F32), 0.0)
        d = jnp.dot((a * a).astype(BF16), w2_ref[...], preferred_element_type=F32)
        if first:
            acc_scr[...] = d
        else:
            acc_scr[...] += d

    pl.when(i == 0)(stage)
    pl.when((i > 0) & (f == 0))(functools.partial(step, True))
    pl.when((i > 0) & (f > 0))(functools.partial(step, False))

    @pl.when((i > 0) & (f == nf - 1))
    def _():
        y = res_ref[...] + g_ref[...] * acc_scr[...]
        if final:
            y = _rms(y, fw_ref[...])
        o_ref[...] = y

    @pl.when(f == nf - 1)
    def _():
        h_cur[...] = h_next[...]


def _ffn(x2, rows_per_cond, nw, sc, sh, gate, w1, w2, layer, final_w):
    m = x2.shape[0]
    tm, tf = 512, 1024
    nt, nf = m // tm, D_FF // tf
    staged = lambda i: jnp.minimum(i, nt - 1)
    done = lambda i: jnp.maximum(i - 1, 0)
    vec = pl.BlockSpec((1, D_MODEL), lambda i, f: (0, 0))
    cvec = lambda tile: pl.BlockSpec((None, 1, D_MODEL),
                                     lambda i, f: ((tile(i) * tm) // rows_per_cond, 0, 0))
    in_specs = [pl.BlockSpec((tm, D_MODEL), lambda i, f: (staged(i), 0)),
                pl.BlockSpec((tm, D_MODEL), lambda i, f: (done(i), 0)),
                vec, cvec(staged), cvec(staged), cvec(done),
                pl.BlockSpec((None, D_MODEL, tf), lambda i, f: (layer, 0, f)),
                pl.BlockSpec((None, tf, D_MODEL), lambda i, f: (layer, f, 0))]
    args = [x2, x2, nw, sc, sh, gate, w1, w2]
    if final_w is not None:
        in_specs.append(vec)
        args.append(final_w)
    return pl.pallas_call(
        functools.partial(_ffn_kernel, final=final_w is not None, nf=nf),
        grid=(nt + 1, nf),
        in_specs=in_specs,
        out_specs=pl.BlockSpec((tm, D_MODEL), lambda i, f: (done(i), 0)),
        out_shape=jax.ShapeDtypeStruct((m, D_MODEL), F32),
        scratch_shapes=[pltpu.VMEM((tm, D_MODEL), BF16), pltpu.VMEM((tm, D_MODEL), BF16),
                        pltpu.VMEM((tm, D_MODEL), F32)],
        compiler_params=_params(("arbitrary", "arbitrary")),
        name="ffn",
    )(*args)


def _rope_tables(seq):
    pos = jnp.arange(seq)
    row = (pos // GRID_W).astype(F32)
    col = (pos % GRID_W).astype(F32)
    half = DK_RET // 2
    inv = 1.0 / (ROPE_BASE ** (jnp.arange(0, half, 2, dtype=F32) / half))
    ang = jnp.concatenate([row[:, None] * inv, col[:, None] * inv], -1)
    cs, sn = jnp.cos(ang), jnp.sin(ang)
    return jnp.concatenate([cs, cs], -1), jnp.concatenate([-sn, sn], -1)


def _quad_lanes(v):
    pad = jnp.zeros((DT_LANES - 4 * HQ,), v.dtype)
    quads = []
    for q in range(NQ):
        fb = [v[0, q * HQ:(q + 1) * HQ], v[1, q * HQ:(q + 1) * HQ]]
        quads.append(jnp.concatenate(fb + fb + [pad]))
    return jnp.stack(quads)[:, None, :]


def _dt_weight(w_dt):
    parts = []
    for q in range(NQ):
        fb = [w_dt[:, q * HQ:(q + 1) * HQ], w_dt[:, H_SSD + q * HQ:H_SSD + (q + 1) * HQ]]
        parts += fb + fb
    parts.append(jnp.zeros((D_MODEL, DT_LANES - NQ * QUAD_LANES), w_dt.dtype))
    return jnp.concatenate(parts, axis=1)


def kernel(x_prompt, x_sample, state_ret, state_ssd, c, c_ctx, w_ada, b_ada, norm1_w, w_in,
           ret_log_decay, conv_w, conv_b, dt_bias, a_log, d_skip, ssd_norm_w, w_out, norm2_w,
           w_ff1, w_ff2, final_norm_w):
    bp, sp, _ = x_prompt.shape
    bs, ss, _ = x_sample.shape

    cond = jnp.zeros((COND_ROWS, D_MODEL), F32).at[:bs].set(c).at[bs].set(c_ctx)
    mod = _ada(cond, w_ada, b_ada).reshape(DEPTH, COND_ROWS, N_MOD, D_MODEL)

    rope = _rope_tables(ss)
    final_w = final_norm_w.reshape(1, D_MODEL)
    w_in_b = w_in.astype(BF16)
    w_dt_b = [_dt_weight(w_in[l, :, D_MAIN:]).astype(BF16) for l in range(DEPTH)]
    w_out_b = w_out.astype(BF16)
    w_ff1_b = w_ff1.astype(BF16)
    w_ff2_b = w_ff2.astype(BF16)

    def run_group(x, mod_rows, rope_tabs, states, want_state):
        b, seq, _ = x.shape
        x2 = x.reshape(b * seq, D_MODEL)
        rows_per_cond = seq if mod_rows.stop - mod_rows.start > 1 else b * seq
        ret_states, ssd_states = [], []
        for l in range(DEPTH):
            mv = [mod[l, mod_rows, k][:, None, :] for k in range(N_MOD)]
            sh1, sc1, g1, sh2, sc2, g2 = mv
            proj, dt = _in_proj(x2, rows_per_cond, norm1_w[l].reshape(1, D_MODEL), sc1, sh1,
                                w_in_b, l, w_dt_b[l])
            proj3 = proj.reshape(b, seq, D_MAIN)
            dt3 = dt.reshape(b, seq, NQ * DT_LANES)
            init_r, init_s = (None, None) if states is None else states
            ret_in = proj3 if states is not None else proj3.reshape(1, b * seq, D_MAIN)
            ret = _retention(ret_in, ret_log_decay[l], rope_tabs, init_r, l, want_state,
                             n_seq=1 if states is not None else b)
            ssd = _ssd(proj3, dt3, d_skip[l], _quad_lanes(dt_bias[l]), _quad_lanes(a_log[l]),
                       conv_w[l], conv_b[l].reshape(1, D_XBC), init_s, l, want_state)
            if want_state:
                ret_states.append(ret[1])
                ssd_states.append(ssd[1])
            x2 = _out_proj(ret[0].reshape(b * seq, D_RET), ssd[0].reshape(b * seq, D_SSD),
                           ssd_norm_w[l].reshape(1, D_SSD), x2, rows_per_cond, g1, w_out_b, l)
            x2 = _ffn(x2, rows_per_cond, norm2_w[l].reshape(1, D_MODEL), sc2, sh2, g2,
                      w_ff1_b, w_ff2_b, l, final_w if l == DEPTH - 1 else None)
        return x2.reshape(b, seq, D_MODEL), ret_states, ssd_states

    y_prompt, ret_states, ssd_states = run_group(x_prompt, slice(bs, bs + 1), None, None, True)
    y_sample, _, _ = run_group(x_sample, slice(0, bs), rope, (state_ret, state_ssd), False)
    return (y_prompt, y_sample, jnp.stack(ret_states, 1), jnp.stack(ssd_states, 1))
```

```python
import functools

import jax
import jax.numpy as jnp
from jax import lax
from jax.experimental import pallas as pl
from jax.experimental.pallas import tpu as pltpu

F32 = jnp.float32
BF16 = jnp.bfloat16

D_MODEL = 2048
DEPTH = 4
CHUNK = 128
H_RET = 8
DK_RET = 128
DV_RET = 128
D_RET = H_RET * DV_RET
H_SSD = 16
P_SSD = 64
D_SSD = H_SSD * P_SSD
G_SSD = 2
N_SSD = 128
D_CONV = 3
D_XBC = D_SSD + 2 * G_SSD * N_SSD
D_FF = 4 * D_MODEL
N_MOD = 6
EPS = 1e-6
ROPE_BASE = 10000.0
GRID_W = 64

D_MAIN = 4 * D_RET + D_SSD + D_XBC
OFF_Q, OFF_K, OFF_V, OFF_G = 0, D_RET, 2 * D_RET, 3 * D_RET
OFF_Z = 4 * D_RET
OFF_XS = OFF_Z + D_SSD
OFF_B = OFF_XS + D_SSD
OFF_C = OFF_B + G_SSD * N_SSD

HQ = 4
NQ = H_SSD // HQ
HP = HQ // 2
DT_LANES = 128
QUAD_LANES = 4 * HQ
COND_ROWS = 16
SCAN_UNROLL = 8
SSD_UNROLL = 2
V7X_VMEM_LIMIT = 56 * 1024 * 1024


def _silu(x):
    return x * jax.nn.sigmoid(x)


def _softplus(x):
    return jnp.maximum(x, 0.0) + jnp.log1p(jnp.exp(-jnp.abs(x)))


def _rms(x, w):
    return x * lax.rsqrt(jnp.mean(x * x, axis=-1, keepdims=True) + EPS) * w


def _params(sem):
    return pltpu.CompilerParams(dimension_semantics=sem, vmem_limit_bytes=V7X_VMEM_LIMIT)


def _ada_kernel(cond_ref, w_ref, b_ref, o_ref):
    s = _silu(cond_ref[...]).astype(BF16)
    o_ref[...] = jnp.dot(s, w_ref[...].astype(BF16), preferred_element_type=F32) + b_ref[...]


def _ada(cond, w_ada, b_ada):
    tn = 1024
    n = N_MOD * D_MODEL
    return pl.pallas_call(
        _ada_kernel,
        grid=(DEPTH, n // tn),
        in_specs=[
            pl.BlockSpec((COND_ROWS, D_MODEL), lambda l, j: (0, 0)),
            pl.BlockSpec((None, D_MODEL, tn), lambda l, j: (l, 0, j)),
            pl.BlockSpec((None, 1, tn), lambda l, j: (l, 0, j)),
        ],
        out_specs=pl.BlockSpec((None, COND_ROWS, tn), lambda l, j: (l, 0, j)),
        out_shape=jax.ShapeDtypeStruct((DEPTH, COND_ROWS, n), F32),
        compiler_params=_params(("arbitrary", "arbitrary")),
        name="ada_mod",
    )(cond, w_ada, b_ada.reshape(DEPTH, 1, n))


def _modulated_norm(x, nw, sc, sh):
    return (_rms(x, nw) * (1.0 + sc) + sh).astype(BF16)


def _in_kernel(x_ref, nw_ref, sc_ref, sh_ref, w_ref, wdt_ref, o_ref, dt_ref, h_cur, h_next, *, nj):
    i, j = pl.program_id(0), pl.program_id(1)
    part = x_ref.shape[0] // nj
    rows = pl.ds(pl.multiple_of(j * part, part), part)

    def stage():
        h_next[rows, :] = _modulated_norm(x_ref[rows, :], nw_ref[...], sc_ref[...], sh_ref[...])

    @pl.when(i == 0)
    def _():
        stage()

    @pl.when(i > 0)
    def _():
        stage()
        dt = jnp.dot(h_cur[rows, :], wdt_ref[...], preferred_element_type=F32)
        for q in range(NQ):
            shift = (DT_LANES - QUAD_LANES * q) % DT_LANES
            dt_ref[rows, q * DT_LANES:(q + 1) * DT_LANES] = pltpu.roll(dt, shift, 1) if shift else dt
        o_ref[...] = jnp.dot(h_cur[...], w_ref[...], preferred_element_type=F32).astype(o_ref.dtype)

    @pl.when(j == nj - 1)
    def _():
        h_cur[...] = h_next[...]


def _in_proj(x2, rows_per_cond, nw, sc, sh, w_in_all, layer, w_dt):
    m = x2.shape[0]
    tm, tn = 512, D_MAIN // 2
    nt = m // tm
    staged = lambda i: jnp.minimum(i, nt - 1)
    done = lambda i: jnp.maximum(i - 1, 0)
    cond_map = lambda i, j: ((staged(i) * tm) // rows_per_cond, 0, 0)
    return pl.pallas_call(
        functools.partial(_in_kernel, nj=D_MAIN // tn),
        grid=(nt + 1, D_MAIN // tn),
        in_specs=[
            pl.BlockSpec((tm, D_MODEL), lambda i, j: (staged(i), 0)),
            pl.BlockSpec((1, D_MODEL), lambda i, j: (0, 0)),
            pl.BlockSpec((None, 1, D_MODEL), cond_map),
            pl.BlockSpec((None, 1, D_MODEL), cond_map),
            pl.BlockSpec((None, D_MODEL, tn), lambda i, j: (layer, 0, jnp.where(i > 0, j, 0))),
            pl.BlockSpec((D_MODEL, DT_LANES), lambda i, j: (0, 0)),
        ],
        out_specs=[
            pl.BlockSpec((tm, tn), lambda i, j: (done(i), jnp.where(i > 0, j, 0))),
            pl.BlockSpec((tm, NQ * DT_LANES), lambda i, j: (done(i), 0)),
        ],
        out_shape=[
            jax.ShapeDtypeStruct((m, D_MAIN), BF16),
            jax.ShapeDtypeStruct((m, NQ * DT_LANES), F32),
        ],
        scratch_shapes=[pltpu.VMEM((tm, D_MODEL), BF16), pltpu.VMEM((tm, D_MODEL), BF16)],
        compiler_params=_params(("arbitrary", "arbitrary")),
        name="in_proj",
    )(x2, nw, sc, sh, w_in_all, w_dt)


def _ret_kernel(*refs, seq, n_seq, rope, has_init, want_state):
    it = iter(refs)
    ld_ref, q_ref, k_ref, v_ref, g_ref = (next(it) for _ in range(5))
    cos_ref = sin_ref = init_ref = st_ref = None
    if rope:
        cos_ref, sin_ref = next(it), next(it)
    if has_init:
        init_ref = next(it)
    if want_state:
        next(it)
    y_ref = next(it)
    if want_state:
        st_ref = next(it)
    q_scr, kt_scr, sf_scr, sb_scr = next(it), next(it), next(it), next(it)

    t = CHUNK
    nc = seq // t
    head = pl.program_id(1)
    la_f = ld_ref[0, head]
    la_b = ld_ref[1, head]

    ii = lax.broadcasted_iota(jnp.int32, (t, t), 0)
    jj = lax.broadcasted_iota(jnp.int32, (t, t), 1)
    dist = (ii - jj).astype(F32)
    lower = ii >= jj
    upper = ii <= jj
    dmat = (jnp.where(lower, jnp.exp(la_f * jnp.where(lower, dist, 0.0)), 0.0)
            + jnp.where(upper, jnp.exp(la_b * jnp.where(upper, -dist, 0.0)), 0.0))
    pos = lax.broadcasted_iota(jnp.int32, (t, 1), 0).astype(F32)
    lane = lax.broadcasted_iota(jnp.int32, (1, t), 1).astype(F32)
    e_f = jnp.exp(la_f * (pos + 1.0))
    e_b = jnp.exp(la_b * (t - pos))
    w_f = jnp.exp(la_f * (t - 1.0 - lane))
    w_b = jnp.exp(la_b * lane)
    dec_f = jnp.exp(jnp.full((1, 1), t, F32) * la_f)
    dec_b = jnp.exp(jnp.full((1, 1), t, F32) * la_b)
    scale = DK_RET ** -0.5
    unroll = min(nc, SCAN_UNROLL)

    def rows(c):
        return pl.ds(pl.multiple_of(c * t, t), t)

    def prep(c, carry):
        r = rows(c)
        q = q_ref[r, :].astype(F32)
        k = k_ref[r, :].astype(F32)
        if rope:
            cs, sn = cos_ref[r, :], sin_ref[r, :]
            q = q * cs + pltpu.roll(q, DK_RET // 2, 1) * sn
            k = k * cs + pltpu.roll(k, DK_RET // 2, 1) * sn
        q_scr[r, :] = (q * scale).astype(BF16)
        kt_scr[c] = k.T.astype(BF16)
        return carry

    lax.fori_loop(0, nc, prep, 0, unroll=unroll)

    def chunk_state(c, w_row):
        kw = (kt_scr[c].astype(F32) * w_row).astype(BF16)
        return jnp.dot(kw, v_ref[rows(c), :], preferred_element_type=F32)

    def states(i, carry, base=0, last=nc - 1):
        s_f, s_b = carry
        c_f, c_b = base + i, base + last - i
        sf_scr[c_f] = s_f.astype(BF16)
        sb_scr[c_b] = s_b.astype(BF16)
        return dec_f * s_f + chunk_state(c_f, w_f), dec_b * s_b + chunk_state(c_b, w_b)

    def outs(c, carry):
        r = rows(c)
        q = q_scr[r, :]
        att = jnp.dot(q, kt_scr[c], preferred_element_type=F32)
        qf = q.astype(F32)
        a = jnp.concatenate([(att * dmat).astype(BF16), (qf * e_f).astype(BF16),
                             (qf * e_b).astype(BF16)], axis=1)
        rhs = jnp.concatenate([v_ref[r, :], sf_scr[c], sb_scr[c]], axis=0)
        y = jnp.dot(a, rhs, preferred_element_type=F32)
        yn = y * lax.rsqrt(jnp.mean(y * y, axis=-1, keepdims=True) + EPS)
        y_ref[r, :] = (_silu(g_ref[r, :].astype(F32)) * yn).astype(y_ref.dtype)
        return carry

    zero = jnp.zeros((DK_RET, DV_RET), F32)
    if n_seq == 1:
        s0_f, s0_b = (init_ref[0].T, init_ref[1].T) if has_init else (zero, zero)
        s_f, s_b = lax.fori_loop(0, nc, states, (s0_f, s0_b), unroll=unroll)
        if want_state:
            st_ref[0, 0] = s_f.T
            st_ref[0, 1] = s_b.T
    else:
        assert not has_init and not rope
        per = nc // n_seq

        def sequence(g, carry):
            s_f = s_b = zero
            for u in range(per):
                s_f, s_b = states(u, (s_f, s_b), base=g * per, last=per - 1)
            if want_state:
                st_ref[g, 0] = s_f.T
                st_ref[g, 1] = s_b.T
            return carry

        lax.fori_loop(0, n_seq, sequence, 0, unroll=min(n_seq, max(1, SCAN_UNROLL // per)))
    lax.fori_loop(0, nc, outs, 0, unroll=unroll)


def _retention(proj3, ret_ld, rope_tabs, init, layer, state_buf, n_seq=1):
    b, seq, _ = proj3.shape
    want_state = state_buf is not None
    aliases = {}
    hblk = lambda off: pl.BlockSpec((None, seq, DK_RET), lambda i, h, o=off // DK_RET: (i, 0, o + h))
    in_specs = [pl.BlockSpec(memory_space=pltpu.SMEM), hblk(OFF_Q), hblk(OFF_K), hblk(OFF_V), hblk(OFF_G)]
    args = [ret_ld, proj3, proj3, proj3, proj3]
    if rope_tabs is not None:
        in_specs += [pl.BlockSpec((seq, DK_RET), lambda i, h: (0, 0))] * 2
        args += list(rope_tabs)
    if init is not None:
        in_specs.append(pl.BlockSpec((None, None, 2, None, DV_RET, DK_RET),
                                     lambda i, h: (i, layer, 0, h, 0, 0)))
        args.append(init)
    out_specs = [pl.BlockSpec((None, seq, DV_RET), lambda i, h: (i, 0, h))]
    out_shape = [jax.ShapeDtypeStruct((b, seq, D_RET), BF16)]
    if want_state:
        aliases[len(args)] = 1
        in_specs.append(pl.BlockSpec(memory_space=pl.ANY))
        args.append(state_buf)
        out_specs.append(pl.BlockSpec((n_seq, None, 2, None, DV_RET, DK_RET),
                                      lambda i, h: (i, layer, 0, h, 0, 0)))
        out_shape.append(jax.ShapeDtypeStruct(state_buf.shape, state_buf.dtype))
    return pl.pallas_call(
        functools.partial(_ret_kernel, seq=seq, n_seq=n_seq, rope=rope_tabs is not None,
                          has_init=init is not None, want_state=want_state),
        grid=(b, H_RET),
        in_specs=in_specs,
        out_specs=out_specs,
        out_shape=out_shape,
        scratch_shapes=[pltpu.VMEM((seq, DK_RET), BF16),
                        pltpu.VMEM((seq // CHUNK, DK_RET, CHUNK), BF16),
                        pltpu.VMEM((seq // CHUNK, DK_RET, DV_RET), BF16),
                        pltpu.VMEM((seq // CHUNK, DK_RET, DV_RET), BF16)],
        input_output_aliases=aliases,
        compiler_params=_params(("arbitrary", "arbitrary")),
        name="retention",
    )(*args)


def _conv_silu_chunk(src_ref, w_ref, b_ref, c, nc):
    t = CHUNK
    r0 = pl.multiple_of(c * t, t)
    cur = src_ref[pl.ds(r0, t), :].astype(F32)
    p0 = pl.multiple_of(jnp.maximum(r0 - 16, 0), 16)
    n0 = pl.multiple_of(jnp.minimum(r0 + t, nc * t - 16), 16)
    prev = jnp.where(c > 0, src_ref[pl.ds(p0, 16), :].astype(F32)[15:16, :], 0.0)
    nxt = jnp.where(c < nc - 1, src_ref[pl.ds(n0, 16), :].astype(F32)[0:1, :], 0.0)
    row = lax.broadcasted_iota(jnp.int32, (t, 1), 0)
    xm1 = jnp.where(row == 0, prev, pltpu.roll(cur, 1, 0))
    xp1 = jnp.where(row == t - 1, nxt, pltpu.roll(cur, t - 1, 0))
    return _silu(xm1 * w_ref[0:1, :] + cur * w_ref[1:2, :] + xp1 * w_ref[2:3, :] + b_ref[...])


def _exact_cumsum(tri_b, x):
    hi = x.astype(BF16)
    r1 = x - hi.astype(F32)
    mid = r1.astype(BF16)
    lo = (r1 - mid.astype(F32)).astype(BF16)
    return (jnp.dot(tri_b, hi, preferred_element_type=F32)
            + jnp.dot(tri_b, mid, preferred_element_type=F32)
            + jnp.dot(tri_b, lo, preferred_element_type=F32))


def _ssd_kernel(*refs, seq, has_init, want_state):
    it = iter(refs)
    (dsk_ref, xs_ref, z_ref, b_ref, c_ref, dt_ref, bias_ref, alog_ref,
     wx_ref, bx_ref, wb_ref, bb_ref, wc_ref, bc_ref) = (next(it) for _ in range(14))
    init_ref = st_ref = None
    if has_init:
        init_ref = next(it)
    if want_state:
        next(it)
    u_ref = next(it)
    if want_state:
        st_ref = next(it)
    xc_scr, bt_scr, cc_scr, col_scr, row_scr, sf_scr, sb_scr = (next(it) for _ in range(7))

    t = CHUNK
    nc = seq // t
    quad = pl.program_id(1)
    unroll = min(nc, SSD_UNROLL)

    ii = lax.broadcasted_iota(jnp.int32, (t, t), 0)
    jj = lax.broadcasted_iota(jnp.int32, (t, t), 1)
    lower = ii >= jj
    strict_lower = ii > jj
    diag = ii == jj
    tril_b = jnp.where(lower, 1.0, 0.0).astype(BF16)
    lane = lax.broadcasted_iota(jnp.int32, (1, DT_LANES), 1)
    a_neg = -jnp.exp(alog_ref[...])
    dt_bias = bias_ref[...]

    def rows(c):
        return pl.ds(pl.multiple_of(c * t, t), t)

    def prep_group(c, carry):
        bt_scr[c] = _conv_silu_chunk(b_ref, wb_ref, bb_ref, c, nc).T.astype(BF16)
        cc_scr[rows(c), :] = _conv_silu_chunk(c_ref, wc_ref, bc_ref, c, nc).astype(BF16)
        return carry

    @pl.when(quad % (NQ // G_SSD) == 0)
    def _():
        lax.fori_loop(0, nc, prep_group, 0, unroll=unroll)

    def prep(c, carry):
        r = rows(c)
        xc_scr[r, :] = _conv_silu_chunk(xs_ref, wx_ref, bx_ref, c, nc).astype(BF16)
        dt = _softplus(dt_ref[r, :] + dt_bias)
        da = dt * a_neg
        cum = _exact_cumsum(tril_b, da)
        total = cum[t - 1:t, :]
        decay = jnp.where(lane < HQ, cum, total - cum + da)
        packed = jnp.where(lane < 2 * HQ, decay, dt)
        col_scr[r, :] = packed
        row_scr[c] = packed.T[0:4 * HQ, :]
        return carry

    lax.fori_loop(0, nc, prep, 0, unroll=min(nc, 2 * SSD_UNROLL))

    even = lax.broadcasted_iota(jnp.int32, (1, 2 * P_SSD), 1) < P_SSD

    def pair_cols(k):
        return slice(k * 2 * P_SSD, (k + 1) * 2 * P_SSD)

    def chunk_states(c, first, tot_lane):
        rowf = row_scr[c]
        cum_r = rowf[first:first + HQ, :]
        dt_r = rowf[2 * HQ + first:3 * HQ + first, :]
        tot = cum_r[:, tot_lane:tot_lane + 1]
        w = jnp.exp(tot - cum_r) * dt_r
        dec = jnp.exp(tot)
        btf = bt_scr[c].astype(F32)
        r = rows(c)
        decs, css = [], []
        for k in range(HP):
            x_pair = xc_scr[r, pair_cols(k)]
            cs = [jnp.dot((btf * w[hh:hh + 1, :]).astype(BF16), x_pair, preferred_element_type=F32)
                  for hh in (2 * k, 2 * k + 1)]
            css.append(jnp.where(even, cs[0], cs[1]))
            decs.append(jnp.where(even, dec[2 * k:2 * k + 1, :], dec[2 * k + 1:2 * k + 2, :]))
        return decs, css

    def states(i, carry):
        s_f, s_b = carry
        c_f, c_b = i, nc - 1 - i
        dec_f, cs_f = chunk_states(c_f, 0, t - 1)
        dec_b, cs_b = chunk_states(c_b, HQ, 0)
        new_f, new_b = [], []
        for k in range(HP):
            sf_scr[c_f * HP + k] = s_f[k].astype(BF16)
            sb_scr[c_b * HP + k] = s_b[k].astype(BF16)
            new_f.append(dec_f[k] * s_f[k] + cs_f[k])
            new_b.append(dec_b[k] * s_b[k] + cs_b[k])
        return tuple(new_f), tuple(new_b)

    def head_lhs(colf, rowf, cb, cf, hh):
        fl, bl = hh, HQ + hh
        rc_f = jnp.broadcast_to(colf[:, fl:fl + 1], (t, t))
        rc_b = jnp.broadcast_to(colf[:, bl:bl + 1], (t, t))
        seg = jnp.where(lower, rc_f - rowf[fl:fl + 1, :], rc_b - rowf[bl:bl + 1, :])
        dt_f = rowf[2 * HQ + fl:2 * HQ + fl + 1, :]
        dt_b = rowf[2 * HQ + bl:2 * HQ + bl + 1, :]
        dt_sel = jnp.where(strict_lower, dt_f, jnp.where(diag, dt_f + dt_b, dt_b))
        return jnp.concatenate([(cb * (jnp.exp(seg) * dt_sel)).astype(BF16),
                                (cf * jnp.exp(rc_f)).astype(BF16),
                                (cf * jnp.exp(rc_b)).astype(BF16)], axis=1)

    def outs(c, carry):
        r = rows(c)
        colf = col_scr[r, :]
        rowf = row_scr[c]
        cmat = cc_scr[r, :]
        cf = cmat.astype(F32)
        cb = jnp.dot(cmat, bt_scr[c], preferred_element_type=F32)
        for k in range(HP):
            x_pair = xc_scr[r, pair_cols(k)]
            rhs = jnp.concatenate([x_pair, sf_scr[c * HP + k], sb_scr[c * HP + k]], axis=0)
            ys = [jnp.dot(head_lhs(colf, rowf, cb, cf, hh), rhs, preferred_element_type=F32)
                  for hh in (2 * k, 2 * k + 1)]
            h0 = quad * HQ + 2 * k
            skip = jnp.where(even, dsk_ref[h0], dsk_ref[h0 + 1])
            y = jnp.where(even, ys[0], ys[1]) + skip * x_pair.astype(F32)
            u_ref[r, pair_cols(k)] = (y * _silu(z_ref[r, pair_cols(k)].astype(F32))).astype(u_ref.dtype)
        return carry

    def pair_state(ref, d, k):
        return ref[d, 2 * k:2 * k + 2].reshape(2 * P_SSD, N_SSD).T

    if has_init:
        s0_f = tuple(pair_state(init_ref, 0, k) for k in range(HP))
        s0_b = tuple(pair_state(init_ref, 1, k) for k in range(HP))
    else:
        s0_f = s0_b = tuple(jnp.zeros((N_SSD, 2 * P_SSD), F32) for _ in range(HP))
    s_f, s_b = lax.fori_loop(0, nc, states, (s0_f, s0_b), unroll=min(nc, 2 * SSD_UNROLL))
    lax.fori_loop(0, nc, outs, 0, unroll=min(nc, 2 * SSD_UNROLL))
    if want_state:
        for k in range(HP):
            st_ref[0, 2 * k:2 * k + 2] = s_f[k].T.reshape(2, P_SSD, N_SSD)
            st_ref[1, 2 * k:2 * k + 2] = s_b[k].T.reshape(2, P_SSD, N_SSD)


def _ssd(proj3, dt3, d_skip, dt_bias_q, a_log_q, conv_w, conv_b, init, layer, state_buf):
    b, seq, _ = proj3.shape
    want_state = state_buf is not None
    aliases = {}
    wq = HQ * P_SSD
    col = lambda width, off, fn: pl.BlockSpec((None, seq, width),
                                              lambda i, q, o=off // width: (i, 0, o + fn(q)))
    ident = lambda q: q
    group = lambda q: q // (NQ // G_SSD)
    xoff = 0
    boff = D_SSD
    coff = D_SSD + G_SSD * N_SSD
    cw = lambda width, off, fn: pl.BlockSpec((D_CONV, width), lambda i, q, o=off // width: (0, o + fn(q)))
    cbias = lambda width, off, fn: pl.BlockSpec((1, width), lambda i, q, o=off // width: (0, o + fn(q)))
    in_specs = [
        pl.BlockSpec(memory_space=pltpu.SMEM),
        col(wq, OFF_XS, ident), col(wq, OFF_Z, ident), col(N_SSD, OFF_B, group), col(N_SSD, OFF_C, group),
        pl.BlockSpec((None, seq, DT_LANES), lambda i, q: (i, 0, q)),
        pl.BlockSpec((None, 1, DT_LANES), lambda i, q: (q, 0, 0)),
        pl.BlockSpec((None, 1, DT_LANES), lambda i, q: (q, 0, 0)),
        cw(wq, xoff, ident), cbias(wq, xoff, ident),
        cw(N_SSD, boff, group), cbias(N_SSD, boff, group),
        cw(N_SSD, coff, group), cbias(N_SSD, coff, group),
    ]
    args = [d_skip, proj3, proj3, proj3, proj3, dt3, dt_bias_q, a_log_q,
            conv_w, conv_b, conv_w, conv_b, conv_w, conv_b]
    st_spec = pl.BlockSpec((None, None, 2, HQ, P_SSD, N_SSD), lambda i, q: (i, layer, 0, q, 0, 0))
    if init is not None:
        in_specs.append(st_spec)
        args.append(init)
    out_specs = [pl.BlockSpec((None, seq, wq), lambda i, q: (i, 0, q))]
    out_shape = [jax.ShapeDtypeStruct((b, seq, D_SSD), BF16)]
    if want_state:
        aliases[len(args)] = 1
        in_specs.append(pl.BlockSpec(memory_space=pl.ANY))
        args.append(state_buf)
        out_specs.append(st_spec)
        out_shape.append(jax.ShapeDtypeStruct(state_buf.shape, state_buf.dtype))
    nc = seq // CHUNK
    return pl.pallas_call(
        functools.partial(_ssd_kernel, seq=seq, has_init=init is not None, want_state=want_state),
        grid=(b, NQ),
        in_specs=in_specs,
        out_specs=out_specs,
        out_shape=out_shape,
        scratch_shapes=[
            pltpu.VMEM((seq, wq), BF16), pltpu.VMEM((nc, N_SSD, CHUNK), BF16), pltpu.VMEM((seq, N_SSD), BF16),
            pltpu.VMEM((seq, DT_LANES), F32), pltpu.VMEM((nc, 4 * HQ, CHUNK), F32),
            pltpu.VMEM((nc * HP, N_SSD, 2 * P_SSD), BF16), pltpu.VMEM((nc * HP, N_SSD, 2 * P_SSD), BF16),
        ],
        input_output_aliases=aliases,
        compiler_params=_params(("arbitrary", "arbitrary")),
        name="ssd",
    )(*args)


def _out_kernel(yr_ref, us_ref, nw_ref, x_ref, g_ref, w_ref, o_ref, a_scr):
    @pl.when(pl.program_id(1) == 0)
    def _():
        a_scr[:, :D_RET] = yr_ref[...]
        a_scr[:, D_RET:] = _rms(us_ref[...].astype(F32), nw_ref[...]).astype(BF16)

    acc = jnp.dot(a_scr[...], w_ref[...], preferred_element_type=F32)
    o_ref[...] = x_ref[...] + g_ref[...] * acc


def _out_proj(yr, us, ssd_nw, x2, rows_per_cond, gate, w_out_all, layer):
    m = x2.shape[0]
    tm, tn = 1024, 1024
    return pl.pallas_call(
        _out_kernel,
        grid=(m // tm, D_MODEL // tn),
        in_specs=[
            pl.BlockSpec((tm, D_RET), lambda i, j: (i, 0)),
            pl.BlockSpec((tm, D_SSD), lambda i, j: (i, 0)),
            pl.BlockSpec((1, D_SSD), lambda i, j: (0, 0)),
            pl.BlockSpec((tm, tn), lambda i, j: (i, j)),
            pl.BlockSpec((None, 1, tn), lambda i, j: ((i * tm) // rows_per_cond, 0, j)),
            pl.BlockSpec((None, D_RET + D_SSD, tn), lambda i, j: (layer, 0, j)),
        ],
        out_specs=pl.BlockSpec((tm, tn), lambda i, j: (i, j)),
        out_shape=jax.ShapeDtypeStruct((m, D_MODEL), F32),
        scratch_shapes=[pltpu.VMEM((tm, D_RET + D_SSD), BF16)],
        compiler_params=_params(("arbitrary", "arbitrary")),
        name="out_proj",
    )(yr, us, ssd_nw, x2, gate, w_out_all)


def _ffn_kernel(*refs, final, nf):
    it = iter(refs)
    x_ref, res_ref, nw_ref, sc_ref, sh_ref, g_ref, w1_ref, w2_ref = (next(it) for _ in range(8))
    fw_ref = next(it) if final else None
    o_ref, h_cur, h_next, acc_scr = next(it), next(it), next(it), next(it)
    i, f = pl.program_id(0), pl.program_id(1)
    part = x_ref.shape[0] // nf
    rows = pl.ds(pl.multiple_of(f * part, part), part)

    def stage():
        h_next[rows, :] = _modulated_norm(x_ref[rows, :], nw_ref[...], sc_ref[...], sh_ref[...])

    def step(first):
        stage()
        a = jnp.maximum(jnp.dot(h_cur[...], w1_ref[...], preferred_element_type=F32), 0.0)
        d = jnp.dot((a * a).astype(BF16), w2_ref[...], preferred_element_type=F32)
        if first:
            acc_scr[...] = d
        else:
            acc_scr[...] += d

    pl.when(i == 0)(stage)
    pl.when((i > 0) & (f == 0))(functools.partial(step, True))
    pl.when((i > 0) & (f > 0))(functools.partial(step, False))

    @pl.when((i > 0) & (f == nf - 1))
    def _():
        y = res_ref[...] + g_ref[...] * acc_scr[...]
        if final:
            y = _rms(y, fw_ref[...])
        o_ref[...] = y

    @pl.when(f == nf - 1)
    def _():
        h_cur[...] = h_next[...]


def _ffn(x2, rows_per_cond, nw, sc, sh, gate, w1, w2, layer, final_w):
    m = x2.shape[0]
    tm, tf = 512, 1024
    nt, nf = m // tm, D_FF // tf
    staged = lambda i: jnp.minimum(i, nt - 1)
    done = lambda i: jnp.maximum(i - 1, 0)
    wtile = lambda i, f: jnp.where(i > 0, f, 0)
    vec = pl.BlockSpec((1, D_MODEL), lambda i, f: (0, 0))
    cvec = lambda tile: pl.BlockSpec((None, 1, D_MODEL),
                                     lambda i, f: ((tile(i) * tm) // rows_per_cond, 0, 0))
    in_specs = [pl.BlockSpec((tm, D_MODEL), lambda i, f: (staged(i), 0)),
                pl.BlockSpec((tm, D_MODEL), lambda i, f: (done(i), 0)),
                vec, cvec(staged), cvec(staged), cvec(done),
                pl.BlockSpec((None, D_MODEL, tf), lambda i, f: (layer, 0, wtile(i, f))),
                pl.BlockSpec((None, tf, D_MODEL), lambda i, f: (layer, wtile(i, f), 0))]
    args = [x2, x2, nw, sc, sh, gate, w1, w2]
    if final_w is not None:
        in_specs.append(vec)
        args.append(final_w)
    return pl.pallas_call(
        functools.partial(_ffn_kernel, final=final_w is not None, nf=nf),
        grid=(nt + 1, nf),
        in_specs=in_specs,
        out_specs=pl.BlockSpec((tm, D_MODEL), lambda i, f: (done(i), 0)),
        out_shape=jax.ShapeDtypeStruct((m, D_MODEL), F32),
        scratch_shapes=[pltpu.VMEM((tm, D_MODEL), BF16), pltpu.VMEM((tm, D_MODEL), BF16),
                        pltpu.VMEM((tm, D_MODEL), F32)],
        compiler_params=_params(("arbitrary", "arbitrary")),
        name="ffn",
    )(*args)


def _rope_tables(seq):
    pos = jnp.arange(seq)
    row = (pos // GRID_W).astype(F32)
    col = (pos % GRID_W).astype(F32)
    half = DK_RET // 2
    inv = 1.0 / (ROPE_BASE ** (jnp.arange(0, half, 2, dtype=F32) / half))
    ang = jnp.concatenate([row[:, None] * inv, col[:, None] * inv], -1)
    cs, sn = jnp.cos(ang), jnp.sin(ang)
    return jnp.concatenate([cs, cs], -1), jnp.concatenate([-sn, sn], -1)


def _quad_lanes(v):
    pad = jnp.zeros((DT_LANES - 4 * HQ,), v.dtype)
    quads = []
    for q in range(NQ):
        fb = [v[0, q * HQ:(q + 1) * HQ], v[1, q * HQ:(q + 1) * HQ]]
        quads.append(jnp.concatenate(fb + fb + [pad]))
    return jnp.stack(quads)[:, None, :]


def _dt_weight(w_dt):
    parts = []
    for q in range(NQ):
        fb = [w_dt[:, q * HQ:(q + 1) * HQ], w_dt[:, H_SSD + q * HQ:H_SSD + (q + 1) * HQ]]
        parts += fb + fb
    parts.append(jnp.zeros((D_MODEL, DT_LANES - NQ * QUAD_LANES), w_dt.dtype))
    return jnp.concatenate(parts, axis=1)


def kernel(x_prompt, x_sample, state_ret, state_ssd, c, c_ctx, w_ada, b_ada, norm1_w, w_in,
           ret_log_decay, conv_w, conv_b, dt_bias, a_log, d_skip, ssd_norm_w, w_out, norm2_w,
           w_ff1, w_ff2, final_norm_w):
    bp, sp, _ = x_prompt.shape
    bs, ss, _ = x_sample.shape

    cond = jnp.zeros((COND_ROWS, D_MODEL), F32).at[:bs].set(c).at[bs].set(c_ctx)
    mod = _ada(cond, w_ada, b_ada).reshape(DEPTH, COND_ROWS, N_MOD, D_MODEL)

    rope = _rope_tables(ss)
    final_w = final_norm_w.reshape(1, D_MODEL)
    w_in_b = w_in.astype(BF16)
    w_dt_b = [_dt_weight(w_in[l, :, D_MAIN:]).astype(BF16) for l in range(DEPTH)]
    w_out_b = w_out.astype(BF16)
    w_ff1_b = w_ff1.astype(BF16)
    w_ff2_b = w_ff2.astype(BF16)

    def run_group(x, mod_rows, rope_tabs, states, new_states):
        b, seq, _ = x.shape
        x2 = x.reshape(b * seq, D_MODEL)
        rows_per_cond = seq if mod_rows.stop - mod_rows.start > 1 else b * seq
        new_ret, new_ssd = new_states
        for l in range(DEPTH):
            mv = [mod[l, mod_rows, k][:, None, :] for k in range(N_MOD)]
            sh1, sc1, g1, sh2, sc2, g2 = mv
            proj, dt = _in_proj(x2, rows_per_cond, norm1_w[l].reshape(1, D_MODEL), sc1, sh1,
                                w_in_b, l, w_dt_b[l])
            proj3 = proj.reshape(b, seq, D_MAIN)
            dt3 = dt.reshape(b, seq, NQ * DT_LANES)
            init_r, init_s = (None, None) if states is None else states
            ret_in = proj3 if states is not None else proj3.reshape(1, b * seq, D_MAIN)
            ret = _retention(ret_in, ret_log_decay[l], rope_tabs, init_r, l, new_ret,
                             n_seq=1 if states is not None else b)
            ssd = _ssd(proj3, dt3, d_skip[l], _quad_lanes(dt_bias[l]), _quad_lanes(a_log[l]),
                       conv_w[l], conv_b[l].reshape(1, D_XBC), init_s, l, new_ssd)
            if new_ret is not None:
                new_ret, new_ssd = ret[1], ssd[1]
            x2 = _out_proj(ret[0].reshape(b * seq, D_RET), ssd[0].reshape(b * seq, D_SSD),
                           ssd_norm_w[l].reshape(1, D_SSD), x2, rows_per_cond, g1, w_out_b, l)
            x2 = _ffn(x2, rows_per_cond, norm2_w[l].reshape(1, D_MODEL), sc2, sh2, g2,
                      w_ff1_b, w_ff2_b, l, final_w if l == DEPTH - 1 else None)
        return x2.reshape(b, seq, D_MODEL), new_ret, new_ssd

    empty_states = (jnp.zeros((bp,) + state_ret.shape[1:], F32), jnp.zeros((bp,) + state_ssd.shape[1:], F32))
    y_prompt, new_state_ret, new_state_ssd = run_group(x_prompt, slice(bs, bs + 1), None, None, empty_states)
    y_sample, _, _ = run_group(x_sample, slice(0, bs), rope, (state_ret, state_ssd), (None, None))
    return (y_prompt, y_sample, new_state_ret, new_state_ssd)
```

```python
import functools

import jax
import jax.numpy as jnp
from jax import lax
from jax.experimental import pallas as pl
from jax.experimental.pallas import tpu as pltpu

F32 = jnp.float32
BF16 = jnp.bfloat16

D_MODEL = 2048
DEPTH = 4
CHUNK = 128
H_RET = 8
DK_RET = 128
DV_RET = 128
D_RET = H_RET * DV_RET
H_SSD = 16
P_SSD = 64
D_SSD = H_SSD * P_SSD
G_SSD = 2
N_SSD = 128
D_CONV = 3
D_XBC = D_SSD + 2 * G_SSD * N_SSD
D_FF = 4 * D_MODEL
N_MOD = 6
EPS = 1e-6
ROPE_BASE = 10000.0
GRID_W = 64

D_MAIN = 4 * D_RET + D_SSD + D_XBC
OFF_Q, OFF_K, OFF_V, OFF_G = 0, D_RET, 2 * D_RET, 3 * D_RET
OFF_Z = 4 * D_RET
OFF_XS = OFF_Z + D_SSD
OFF_B = OFF_XS + D_SSD
OFF_C = OFF_B + G_SSD * N_SSD

HQ = 4
NQ = H_SSD // HQ
HP = HQ // 2
DT_LANES = 128
QUAD_LANES = 4 * HQ
COND_ROWS = 16
SCAN_UNROLL = 16
SSD_UNROLL = 2
V7X_VMEM_LIMIT = 56 * 1024 * 1024


def _silu(x):
    return x * jax.nn.sigmoid(x)


def _softplus(x):
    return jnp.maximum(x, 0.0) + jnp.log1p(jnp.exp(-jnp.abs(x)))


def _rms(x, w):
    return x * lax.rsqrt(jnp.mean(x * x, axis=-1, keepdims=True) + EPS) * w


def _params(sem):
    return pltpu.CompilerParams(dimension_semantics=sem, vmem_limit_bytes=V7X_VMEM_LIMIT)


def _ada_kernel(cond_ref, w_ref, b_ref, o_ref):
    s = _silu(cond_ref[...]).astype(BF16)
    o_ref[...] = jnp.dot(s, w_ref[...].astype(BF16), preferred_element_type=F32) + b_ref[...]


def _ada(cond, w_ada, b_ada):
    tn = 1024
    n = N_MOD * D_MODEL
    return pl.pallas_call(
        _ada_kernel,
        grid=(DEPTH, n // tn),
        in_specs=[
            pl.BlockSpec((COND_ROWS, D_MODEL), lambda l, j: (0, 0)),
            pl.BlockSpec((None, D_MODEL, tn), lambda l, j: (l, 0, j)),
            pl.BlockSpec((None, 1, tn), lambda l, j: (l, 0, j)),
        ],
        out_specs=pl.BlockSpec((None, COND_ROWS, tn), lambda l, j: (l, 0, j)),
        out_shape=jax.ShapeDtypeStruct((DEPTH, COND_ROWS, n), F32),
        compiler_params=_params(("arbitrary", "arbitrary")),
        name="ada_mod",
    )(cond, w_ada, b_ada.reshape(DEPTH, 1, n))


def _modulated_norm(x, nw, sc, sh):
    return (_rms(x, nw) * (1.0 + sc) + sh).astype(BF16)


def _in_kernel(x_ref, nw_ref, sc_ref, sh_ref, w_ref, wdt_ref, o_ref, dt_ref, h_cur, h_next, *, nj):
    i, j = pl.program_id(0), pl.program_id(1)
    part = x_ref.shape[0] // nj
    rows = pl.ds(pl.multiple_of(j * part, part), part)

    def stage():
        h_next[rows, :] = _modulated_norm(x_ref[rows, :], nw_ref[...], sc_ref[...], sh_ref[...])

    @pl.when(i == 0)
    def _():
        stage()

    @pl.when(i > 0)
    def _():
        stage()
        dt = jnp.dot(h_cur[rows, :], wdt_ref[...], preferred_element_type=F32)
        for q in range(NQ):
            shift = (DT_LANES - QUAD_LANES * q) % DT_LANES
            dt_ref[rows, q * DT_LANES:(q + 1) * DT_LANES] = pltpu.roll(dt, shift, 1) if shift else dt
        o_ref[...] = jnp.dot(h_cur[...], w_ref[...], preferred_element_type=F32).astype(o_ref.dtype)

    @pl.when(j == nj - 1)
    def _():
        h_cur[...] = h_next[...]


def _in_proj(x2, rows_per_cond, nw, sc, sh, w_in_all, layer, w_dt):
    m = x2.shape[0]
    tm, tn = 512, D_MAIN // 2
    nt = m // tm
    staged = lambda i: jnp.minimum(i, nt - 1)
    done = lambda i: jnp.maximum(i - 1, 0)
    cond_map = lambda i, j: ((staged(i) * tm) // rows_per_cond, 0, 0)
    return pl.pallas_call(
        functools.partial(_in_kernel, nj=D_MAIN // tn),
        grid=(nt + 1, D_MAIN // tn),
        in_specs=[
            pl.BlockSpec((tm, D_MODEL), lambda i, j: (staged(i), 0)),
            pl.BlockSpec((1, D_MODEL), lambda i, j: (0, 0)),
            pl.BlockSpec((None, 1, D_MODEL), cond_map),
            pl.BlockSpec((None, 1, D_MODEL), cond_map),
            pl.BlockSpec((None, D_MODEL, tn), lambda i, j: (layer, 0, jnp.where(i > 0, j, 0))),
            pl.BlockSpec((D_MODEL, DT_LANES), lambda i, j: (0, 0)),
        ],
        out_specs=[
            pl.BlockSpec((tm, tn), lambda i, j: (done(i), jnp.where(i > 0, j, 0))),
            pl.BlockSpec((tm, NQ * DT_LANES), lambda i, j: (done(i), 0)),
        ],
        out_shape=[
            jax.ShapeDtypeStruct((m, D_MAIN), BF16),
            jax.ShapeDtypeStruct((m, NQ * DT_LANES), F32),
        ],
        scratch_shapes=[pltpu.VMEM((tm, D_MODEL), BF16), pltpu.VMEM((tm, D_MODEL), BF16)],
        compiler_params=_params(("arbitrary", "arbitrary")),
        name="in_proj",
    )(x2, nw, sc, sh, w_in_all, w_dt)


def _ret_kernel(*refs, seq, n_seq, rope, has_init, want_state):
    it = iter(refs)
    ld_ref, q_ref, k_ref, v_ref, g_ref = (next(it) for _ in range(5))
    cos_ref = sin_ref = init_ref = st_ref = None
    if rope:
        cos_ref, sin_ref = next(it), next(it)
    if has_init:
        init_ref = next(it)
    if want_state:
        next(it)
    y_ref = next(it)
    if want_state:
        st_ref = next(it)
    q_scr, kt_scr, sf_scr, sb_scr = next(it), next(it), next(it), next(it)

    t = CHUNK
    nc = seq // t
    head = pl.program_id(1)
    la_f = ld_ref[0, head]
    la_b = ld_ref[1, head]

    ii = lax.broadcasted_iota(jnp.int32, (t, t), 0)
    jj = lax.broadcasted_iota(jnp.int32, (t, t), 1)
    dist = (ii - jj).astype(F32)
    lower = ii >= jj
    upper = ii <= jj
    dmat = (jnp.where(lower, jnp.exp(la_f * jnp.where(lower, dist, 0.0)), 0.0)
            + jnp.where(upper, jnp.exp(la_b * jnp.where(upper, -dist, 0.0)), 0.0))
    pos = lax.broadcasted_iota(jnp.int32, (t, 1), 0).astype(F32)
    lane = lax.broadcasted_iota(jnp.int32, (1, t), 1).astype(F32)
    e_f = jnp.exp(la_f * (pos + 1.0))
    e_b = jnp.exp(la_b * (t - pos))
    w_f = jnp.exp(la_f * (t - 1.0 - lane))
    w_b = jnp.exp(la_b * lane)
    dec_f = jnp.exp(jnp.full((1, 1), t, F32) * la_f)
    dec_b = jnp.exp(jnp.full((1, 1), t, F32) * la_b)
    scale = DK_RET ** -0.5
    unroll = min(nc, SCAN_UNROLL)

    def rows(c):
        return pl.ds(pl.multiple_of(c * t, t), t)

    def prep(c, carry):
        r = rows(c)
        q = q_ref[r, :].astype(F32)
        k = k_ref[r, :].astype(F32)
        if rope:
            cs, sn = cos_ref[r, :], sin_ref[r, :]
            q = q * cs + pltpu.roll(q, DK_RET // 2, 1) * sn
            k = k * cs + pltpu.roll(k, DK_RET // 2, 1) * sn
        q_scr[r, :] = (q * scale).astype(BF16)
        kt_scr[c] = k.T.astype(BF16)
        return carry

    lax.fori_loop(0, nc, prep, 0, unroll=unroll)

    def chunk_state(c, w_row):
        kw = (kt_scr[c].astype(F32) * w_row).astype(BF16)
        return jnp.dot(kw, v_ref[rows(c), :], preferred_element_type=F32)

    def states(i, carry, base=0, last=nc - 1):
        s_f, s_b = carry
        c_f, c_b = base + i, base + last - i
        sf_scr[c_f] = s_f.astype(BF16)
        sb_scr[c_b] = s_b.astype(BF16)
        return dec_f * s_f + chunk_state(c_f, w_f), dec_b * s_b + chunk_state(c_b, w_b)

    def outs(c, carry):
        r = rows(c)
        q = q_scr[r, :]
        att = jnp.dot(q, kt_scr[c], preferred_element_type=F32)
        qf = q.astype(F32)
        a = jnp.concatenate([(att * dmat).astype(BF16), (qf * e_f).astype(BF16),
                             (qf * e_b).astype(BF16)], axis=1)
        rhs = jnp.concatenate([v_ref[r, :], sf_scr[c], sb_scr[c]], axis=0)
        y = jnp.dot(a, rhs, preferred_element_type=F32)
        yn = y * lax.rsqrt(jnp.mean(y * y, axis=-1, keepdims=True) + EPS)
        y_ref[r, :] = (_silu(g_ref[r, :].astype(F32)) * yn).astype(y_ref.dtype)
        return carry

    zero = jnp.zeros((DK_RET, DV_RET), F32)
    if n_seq == 1:
        s0_f, s0_b = (init_ref[0].T, init_ref[1].T) if has_init else (zero, zero)
        s_f, s_b = lax.fori_loop(0, nc, states, (s0_f, s0_b), unroll=unroll)
        if want_state:
            st_ref[0, 0] = s_f.T
            st_ref[0, 1] = s_b.T
    else:
        assert not has_init and not rope
        per = nc // n_seq

        def sequence(g, carry):
            s_f = s_b = zero
            for u in range(per):
                s_f, s_b = states(u, (s_f, s_b), base=g * per, last=per - 1)
            if want_state:
                st_ref[g, 0] = s_f.T
                st_ref[g, 1] = s_b.T
            return carry

        lax.fori_loop(0, n_seq, sequence, 0, unroll=min(n_seq, max(1, SCAN_UNROLL // per)))
    lax.fori_loop(0, nc, outs, 0, unroll=unroll)


def _retention(proj3, ret_ld, rope_tabs, init, layer, state_buf, n_seq=1):
    b, seq, _ = proj3.shape
    want_state = state_buf is not None
    aliases = {}
    hblk = lambda off: pl.BlockSpec((None, seq, DK_RET), lambda i, h, o=off // DK_RET: (i, 0, o + h))
    in_specs = [pl.BlockSpec(memory_space=pltpu.SMEM), hblk(OFF_Q), hblk(OFF_K), hblk(OFF_V), hblk(OFF_G)]
    args = [ret_ld, proj3, proj3, proj3, proj3]
    if rope_tabs is not None:
        in_specs += [pl.BlockSpec((seq, DK_RET), lambda i, h: (0, 0))] * 2
        args += list(rope_tabs)
    if init is not None:
        in_specs.append(pl.BlockSpec((None, None, 2, None, DV_RET, DK_RET),
                                     lambda i, h: (i, layer, 0, h, 0, 0)))
        args.append(init)
    out_specs = [pl.BlockSpec((None, seq, DV_RET), lambda i, h: (i, 0, h))]
    out_shape = [jax.ShapeDtypeStruct((b, seq, D_RET), BF16)]
    if want_state:
        aliases[len(args)] = 1
        in_specs.append(pl.BlockSpec(memory_space=pl.ANY))
        args.append(state_buf)
        out_specs.append(pl.BlockSpec((n_seq, None, 2, None, DV_RET, DK_RET),
                                      lambda i, h: (i, layer, 0, h, 0, 0)))
        out_shape.append(jax.ShapeDtypeStruct(state_buf.shape, state_buf.dtype))
    return pl.pallas_call(
        functools.partial(_ret_kernel, seq=seq, n_seq=n_seq, rope=rope_tabs is not None,
                          has_init=init is not None, want_state=want_state),
        grid=(b, H_RET),
        in_specs=in_specs,
        out_specs=out_specs,
        out_shape=out_shape,
        scratch_shapes=[pltpu.VMEM((seq, DK_RET), BF16),
                        pltpu.VMEM((seq // CHUNK, DK_RET, CHUNK), BF16),
                        pltpu.VMEM((seq // CHUNK, DK_RET, DV_RET), BF16),
                        pltpu.VMEM((seq // CHUNK, DK_RET, DV_RET), BF16)],
        input_output_aliases=aliases,
        compiler_params=_params(("arbitrary", "arbitrary")),
        name="retention",
    )(*args)


def _conv_silu_chunk(src_ref, w_ref, b_ref, c, nc):
    t = CHUNK
    r0 = pl.multiple_of(c * t, t)
    cur = src_ref[pl.ds(r0, t), :].astype(F32)
    p0 = pl.multiple_of(jnp.maximum(r0 - 16, 0), 16)
    n0 = pl.multiple_of(jnp.minimum(r0 + t, nc * t - 16), 16)
    prev = jnp.where(c > 0, src_ref[pl.ds(p0, 16), :].astype(F32)[15:16, :], 0.0)
    nxt = jnp.where(c < nc - 1, src_ref[pl.ds(n0, 16), :].astype(F32)[0:1, :], 0.0)
    row = lax.broadcasted_iota(jnp.int32, (t, 1), 0)
    xm1 = jnp.where(row == 0, prev, pltpu.roll(cur, 1, 0))
    xp1 = jnp.where(row == t - 1, nxt, pltpu.roll(cur, t - 1, 0))
    return _silu(xm1 * w_ref[0:1, :] + cur * w_ref[1:2, :] + xp1 * w_ref[2:3, :] + b_ref[...])


def _exact_cumsum(tri_b, x):
    hi = x.astype(BF16)
    r1 = x - hi.astype(F32)
    mid = r1.astype(BF16)
    lo = (r1 - mid.astype(F32)).astype(BF16)
    return (jnp.dot(tri_b, hi, preferred_element_type=F32)
            + jnp.dot(tri_b, mid, preferred_element_type=F32)
            + jnp.dot(tri_b, lo, preferred_element_type=F32))


def _ssd_kernel(*refs, seq, has_init, want_state):
    it = iter(refs)
    (dsk_ref, xs_ref, z_ref, b_ref, c_ref, dt_ref, bias_ref, alog_ref,
     wx_ref, bx_ref, wb_ref, bb_ref, wc_ref, bc_ref) = (next(it) for _ in range(14))
    init_ref = st_ref = None
    if has_init:
        init_ref = next(it)
    if want_state:
        next(it)
    u_ref = next(it)
    if want_state:
        st_ref = next(it)
    xc_scr, bt_scr, cc_scr, col_scr, row_scr, sf_scr, sb_scr = (next(it) for _ in range(7))

    t = CHUNK
    nc = seq // t
    quad = pl.program_id(1)
    unroll = min(nc, SSD_UNROLL)

    ii = lax.broadcasted_iota(jnp.int32, (t, t), 0)
    jj = lax.broadcasted_iota(jnp.int32, (t, t), 1)
    lower = ii >= jj
    strict_lower = ii > jj
    diag = ii == jj
    tril_b = jnp.where(lower, 1.0, 0.0).astype(BF16)
    lane = lax.broadcasted_iota(jnp.int32, (1, DT_LANES), 1)
    a_neg = -jnp.exp(alog_ref[...])
    dt_bias = bias_ref[...]

    def rows(c):
        return pl.ds(pl.multiple_of(c * t, t), t)

    def prep_group(c, carry):
        bt_scr[c] = _conv_silu_chunk(b_ref, wb_ref, bb_ref, c, nc).T.astype(BF16)
        cc_scr[rows(c), :] = _conv_silu_chunk(c_ref, wc_ref, bc_ref, c, nc).astype(BF16)
        return carry

    @pl.when(quad % (NQ // G_SSD) == 0)
    def _():
        lax.fori_loop(0, nc, prep_group, 0, unroll=unroll)

    def prep(c, carry):
        r = rows(c)
        xc_scr[r, :] = _conv_silu_chunk(xs_ref, wx_ref, bx_ref, c, nc).astype(BF16)
        dt = _softplus(dt_ref[r, :] + dt_bias)
        da = dt * a_neg
        cum = _exact_cumsum(tril_b, da)
        total = cum[t - 1:t, :]
        decay = jnp.where(lane < HQ, cum, total - cum + da)
        packed = jnp.where(lane < 2 * HQ, decay, dt)
        col_scr[r, :] = packed
        row_scr[c] = packed.T[0:4 * HQ, :]
        return carry

    lax.fori_loop(0, nc, prep, 0, unroll=min(nc, 2 * SSD_UNROLL))

    even = lax.broadcasted_iota(jnp.int32, (1, 2 * P_SSD), 1) < P_SSD

    def pair_cols(k):
        return slice(k * 2 * P_SSD, (k + 1) * 2 * P_SSD)

    def chunk_states(c, first, tot_lane):
        rowf = row_scr[c]
        cum_r = rowf[first:first + HQ, :]
        dt_r = rowf[2 * HQ + first:3 * HQ + first, :]
        tot = cum_r[:, tot_lane:tot_lane + 1]
        w = jnp.exp(tot - cum_r) * dt_r
        dec = jnp.exp(tot)
        btf = bt_scr[c].astype(F32)
        r = rows(c)
        decs, css = [], []
        for k in range(HP):
            x_pair = xc_scr[r, pair_cols(k)]
            cs = [jnp.dot((btf * w[hh:hh + 1, :]).astype(BF16), x_pair, preferred_element_type=F32)
                  for hh in (2 * k, 2 * k + 1)]
            css.append(jnp.where(even, cs[0], cs[1]))
            decs.append(jnp.where(even, dec[2 * k:2 * k + 1, :], dec[2 * k + 1:2 * k + 2, :]))
        return decs, css

    def states(i, carry):
        s_f, s_b = carry
        c_f, c_b = i, nc - 1 - i
        dec_f, cs_f = chunk_states(c_f, 0, t - 1)
        dec_b, cs_b = chunk_states(c_b, HQ, 0)
        new_f, new_b = [], []
        for k in range(HP):
            sf_scr[c_f * HP + k] = s_f[k].astype(BF16)
            sb_scr[c_b * HP + k] = s_b[k].astype(BF16)
            new_f.append(dec_f[k] * s_f[k] + cs_f[k])
            new_b.append(dec_b[k] * s_b[k] + cs_b[k])
        return tuple(new_f), tuple(new_b)

    def head_lhs(colf, rowf, cb, cf, hh):
        fl, bl = hh, HQ + hh
        rc_f = jnp.broadcast_to(colf[:, fl:fl + 1], (t, t))
        rc_b = jnp.broadcast_to(colf[:, bl:bl + 1], (t, t))
        seg = jnp.where(lower, rc_f - rowf[fl:fl + 1, :], rc_b - rowf[bl:bl + 1, :])
        dt_f = rowf[2 * HQ + fl:2 * HQ + fl + 1, :]
        dt_b = rowf[2 * HQ + bl:2 * HQ + bl + 1, :]
        dt_sel = jnp.where(strict_lower, dt_f, jnp.where(diag, dt_f + dt_b, dt_b))
        return jnp.concatenate([(cb * (jnp.exp(seg) * dt_sel)).astype(BF16),
                                (cf * jnp.exp(rc_f)).astype(BF16),
                                (cf * jnp.exp(rc_b)).astype(BF16)], axis=1)

    def outs(c, carry):
        r = rows(c)
        colf = col_scr[r, :]
        rowf = row_scr[c]
        cmat = cc_scr[r, :]
        cf = cmat.astype(F32)
        cb = jnp.dot(cmat, bt_scr[c], preferred_element_type=F32)
        for k in range(HP):
            x_pair = xc_scr[r, pair_cols(k)]
            rhs = jnp.concatenate([x_pair, sf_scr[c * HP + k], sb_scr[c * HP + k]], axis=0)
            ys = [jnp.dot(head_lhs(colf, rowf, cb, cf, hh), rhs, preferred_element_type=F32)
                  for hh in (2 * k, 2 * k + 1)]
            h0 = quad * HQ + 2 * k
            skip = jnp.where(even, dsk_ref[h0], dsk_ref[h0 + 1])
            y = jnp.where(even, ys[0], ys[1]) + skip * x_pair.astype(F32)
            u_ref[r, pair_cols(k)] = (y * _silu(z_ref[r, pair_cols(k)].astype(F32))).astype(u_ref.dtype)
        return carry

    def pair_state(ref, d, k):
        return ref[d, 2 * k:2 * k + 2].reshape(2 * P_SSD, N_SSD).T

    if has_init:
        s0_f = tuple(pair_state(init_ref, 0, k) for k in range(HP))
        s0_b = tuple(pair_state(init_ref, 1, k) for k in range(HP))
    else:
        s0_f = s0_b = tuple(jnp.zeros((N_SSD, 2 * P_SSD), F32) for _ in range(HP))
    s_f, s_b = lax.fori_loop(0, nc, states, (s0_f, s0_b), unroll=min(nc, 2 * SSD_UNROLL))
    lax.fori_loop(0, nc, outs, 0, unroll=min(nc, 2 * SSD_UNROLL))
    if want_state:
        for k in range(HP):
            st_ref[0, 2 * k:2 * k + 2] = s_f[k].T.reshape(2, P_SSD, N_SSD)
            st_ref[1, 2 * k:2 * k + 2] = s_b[k].T.reshape(2, P_SSD, N_SSD)


def _ssd(proj3, dt3, d_skip, dt_bias_q, a_log_q, conv_w, conv_b, init, layer, state_buf):
    b, seq, _ = proj3.shape
    want_state = state_buf is not None
    aliases = {}
    wq = HQ * P_SSD
    col = lambda width, off, fn: pl.BlockSpec((None, seq, width),
                                              lambda i, q, o=off // width: (i, 0, o + fn(q)))
    ident = lambda q: q
    group = lambda q: q // (NQ // G_SSD)
    xoff = 0
    boff = D_SSD
    coff = D_SSD + G_SSD * N_SSD
    cw = lambda width, off, fn: pl.BlockSpec((D_CONV, width), lambda i, q, o=off // width: (0, o + fn(q)))
    cbias = lambda width, off, fn: pl.BlockSpec((1, width), lambda i, q, o=off // width: (0, o + fn(q)))
    in_specs = [
        pl.BlockSpec(memory_space=pltpu.SMEM),
        col(wq, OFF_XS, ident), col(wq, OFF_Z, ident), col(N_SSD, OFF_B, group), col(N_SSD, OFF_C, group),
        pl.BlockSpec((None, seq, DT_LANES), lambda i, q: (i, 0, q)),
        pl.BlockSpec((None, 1, DT_LANES), lambda i, q: (q, 0, 0)),
        pl.BlockSpec((None, 1, DT_LANES), lambda i, q: (q, 0, 0)),
        cw(wq, xoff, ident), cbias(wq, xoff, ident),
        cw(N_SSD, boff, group), cbias(N_SSD, boff, group),
        cw(N_SSD, coff, group), cbias(N_SSD, coff, group),
    ]
    args = [d_skip, proj3, proj3, proj3, proj3, dt3, dt_bias_q, a_log_q,
            conv_w, conv_b, conv_w, conv_b, conv_w, conv_b]
    st_spec = pl.BlockSpec((None, None, 2, HQ, P_SSD, N_SSD), lambda i, q: (i, layer, 0, q, 0, 0))
    if init is not None:
        in_specs.append(st_spec)
        args.append(init)
    out_specs = [pl.BlockSpec((None, seq, wq), lambda i, q: (i, 0, q))]
    out_shape = [jax.ShapeDtypeStruct((b, seq, D_SSD), BF16)]
    if want_state:
        aliases[len(args)] = 1
        in_specs.append(pl.BlockSpec(memory_space=pl.ANY))
        args.append(state_buf)
        out_specs.append(st_spec)
        out_shape.append(jax.ShapeDtypeStruct(state_buf.shape, state_buf.dtype))
    nc = seq // CHUNK
    return pl.pallas_call(
        functools.partial(_ssd_kernel, seq=seq, has_init=init is not None, want_state=want_state),
        grid=(b, NQ),
        in_specs=in_specs,
        out_specs=out_specs,
        out_shape=out_shape,
        scratch_shapes=[
            pltpu.VMEM((seq, wq), BF16), pltpu.VMEM((nc, N_SSD, CHUNK), BF16), pltpu.VMEM((seq, N_SSD), BF16),
            pltpu.VMEM((seq, DT_LANES), F32), pltpu.VMEM((nc, 4 * HQ, CHUNK), F32),
            pltpu.VMEM((nc * HP, N_SSD, 2 * P_SSD), BF16), pltpu.VMEM((nc * HP, N_SSD, 2 * P_SSD), BF16),
        ],
        input_output_aliases=aliases,
        compiler_params=_params(("arbitrary", "arbitrary")),
        name="ssd",
    )(*args)


def _out_kernel(yr_ref, us_ref, nw_ref, x_ref, g_ref, w_ref, o_ref):
    a = jnp.concatenate([yr_ref[...], _rms(us_ref[...].astype(F32), nw_ref[...]).astype(BF16)], axis=1)
    o_ref[...] = x_ref[...] + g_ref[...] * jnp.dot(a, w_ref[...], preferred_element_type=F32)


def _out_proj(yr, us, ssd_nw, x2, rows_per_cond, gate, w_out_all, layer):
    m = x2.shape[0]
    tm = 512
    return pl.pallas_call(
        _out_kernel,
        grid=(m // tm,),
        in_specs=[
            pl.BlockSpec((tm, D_RET), lambda i: (i, 0)),
            pl.BlockSpec((tm, D_SSD), lambda i: (i, 0)),
            pl.BlockSpec((1, D_SSD), lambda i: (0, 0)),
            pl.BlockSpec((tm, D_MODEL), lambda i: (i, 0)),
            pl.BlockSpec((None, 1, D_MODEL), lambda i: ((i * tm) // rows_per_cond, 0, 0)),
            pl.BlockSpec((None, D_RET + D_SSD, D_MODEL), lambda i: (layer, 0, 0)),
        ],
        out_specs=pl.BlockSpec((tm, D_MODEL), lambda i: (i, 0)),
        out_shape=jax.ShapeDtypeStruct((m, D_MODEL), F32),
        compiler_params=_params(("arbitrary",)),
        name="out_proj",
    )(yr, us, ssd_nw, x2, gate, w_out_all)


def _ffn_kernel(*refs, final, nf):
    it = iter(refs)
    x_ref, res_ref, nw_ref, sc_ref, sh_ref, g_ref, w1_ref, w2_ref = (next(it) for _ in range(8))
    fw_ref = next(it) if final else None
    o_ref, h_cur, h_next, acc_scr = next(it), next(it), next(it), next(it)
    i, f = pl.program_id(0), pl.program_id(1)
    part = x_ref.shape[0] // nf
    rows = pl.ds(pl.multiple_of(f * part, part), part)

    def stage():
        h_next[rows, :] = _modulated_norm(x_ref[rows, :], nw_ref[...], sc_ref[...], sh_ref[...])

    def step(first):
        stage()
        a = jnp.maximum(jnp.dot(h_cur[...], w1_ref[...], preferred_element_type=F32), 0.0)
        d = jnp.dot((a * a).astype(BF16), w2_ref[...], preferred_element_type=F32)
        if first:
            acc_scr[...] = d
        else:
            acc_scr[...] += d

    pl.when(i == 0)(stage)
    pl.when((i > 0) & (f == 0))(functools.partial(step, True))
    pl.when((i > 0) & (f > 0))(functools.partial(step, False))

    @pl.when((i > 0) & (f == nf - 1))
    def _():
        y = res_ref[...] + g_ref[...] * acc_scr[...]
        if final:
            y = _rms(y, fw_ref[...])
        o_ref[...] = y

    @pl.when(f == nf - 1)
    def _():
        h_cur[...] = h_next[...]


def _ffn(x2, rows_per_cond, nw, sc, sh, gate, w1, w2, layer, final_w):
    m = x2.shape[0]
    tm, tf = 512, 1024
    nt, nf = m // tm, D_FF // tf
    staged = lambda i: jnp.minimum(i, nt - 1)
    done = lambda i: jnp.maximum(i - 1, 0)
    wtile = lambda i, f: jnp.where(i > 0, f, 0)
    vec = pl.BlockSpec((1, D_MODEL), lambda i, f: (0, 0))
    cvec = lambda tile: pl.BlockSpec((None, 1, D_MODEL),
                                     lambda i, f: ((tile(i) * tm) // rows_per_cond, 0, 0))
    in_specs = [pl.BlockSpec((tm, D_MODEL), lambda i, f: (staged(i), 0)),
                pl.BlockSpec((tm, D_MODEL), lambda i, f: (done(i), 0)),
                vec, cvec(staged), cvec(staged), cvec(done),
                pl.BlockSpec((None, D_MODEL, tf), lambda i, f: (layer, 0, wtile(i, f))),
                pl.BlockSpec((None, tf, D_MODEL), lambda i, f: (layer, wtile(i, f), 0))]
    args = [x2, x2, nw, sc, sh, gate, w1, w2]
    if final_w is not None:
        in_specs.append(vec)
        args.append(final_w)
    return pl.pallas_call(
        functools.partial(_ffn_kernel, final=final_w is not None, nf=nf),
        grid=(nt + 1, nf),
        in_specs=in_specs,
        out_specs=pl.BlockSpec((tm, D_MODEL), lambda i, f: (done(i), 0)),
        out_shape=jax.ShapeDtypeStruct((m, D_MODEL), F32),
        scratch_shapes=[pltpu.VMEM((tm, D_MODEL), BF16), pltpu.VMEM((tm, D_MODEL), BF16),
                        pltpu.VMEM((tm, D_MODEL), F32)],
        compiler_params=_params(("arbitrary", "arbitrary")),
        name="ffn",
    )(*args)


def _rope_tables(seq):
    pos = jnp.arange(seq)
    row = (pos // GRID_W).astype(F32)
    col = (pos % GRID_W).astype(F32)
    half = DK_RET // 2
    inv = 1.0 / (ROPE_BASE ** (jnp.arange(0, half, 2, dtype=F32) / half))
    ang = jnp.concatenate([row[:, None] * inv, col[:, None] * inv], -1)
    cs, sn = jnp.cos(ang), jnp.sin(ang)
    return jnp.concatenate([cs, cs], -1), jnp.concatenate([-sn, sn], -1)


def _quad_lanes(v):
    pad = jnp.zeros((DT_LANES - 4 * HQ,), v.dtype)
    quads = []
    for q in range(NQ):
        fb = [v[0, q * HQ:(q + 1) * HQ], v[1, q * HQ:(q + 1) * HQ]]
        quads.append(jnp.concatenate(fb + fb + [pad]))
    return jnp.stack(quads)[:, None, :]


def _dt_weight(w_dt):
    parts = []
    for q in range(NQ):
        fb = [w_dt[:, q * HQ:(q + 1) * HQ], w_dt[:, H_SSD + q * HQ:H_SSD + (q + 1) * HQ]]
        parts += fb + fb
    parts.append(jnp.zeros((D_MODEL, DT_LANES - NQ * QUAD_LANES), w_dt.dtype))
    return jnp.concatenate(parts, axis=1)


def kernel(x_prompt, x_sample, state_ret, state_ssd, c, c_ctx, w_ada, b_ada, norm1_w, w_in,
           ret_log_decay, conv_w, conv_b, dt_bias, a_log, d_skip, ssd_norm_w, w_out, norm2_w,
           w_ff1, w_ff2, final_norm_w):
    bp, sp, _ = x_prompt.shape
    bs, ss, _ = x_sample.shape

    cond = jnp.zeros((COND_ROWS, D_MODEL), F32).at[:bs].set(c).at[bs].set(c_ctx)
    mod = _ada(cond, w_ada, b_ada).reshape(DEPTH, COND_ROWS, N_MOD, D_MODEL)

    rope = _rope_tables(ss)
    final_w = final_norm_w.reshape(1, D_MODEL)
    w_in_b = w_in[:, :, :D_MAIN].astype(BF16)
    w_dt_b = [_dt_weight(w_in[l, :, D_MAIN:]).astype(BF16) for l in range(DEPTH)]
    w_out_b = w_out.astype(BF16)
    w_ff1_b = w_ff1.astype(BF16)
    w_ff2_b = w_ff2.astype(BF16)

    def run_group(x, mod_rows, rope_tabs, states, new_states):
        b, seq, _ = x.shape
        x2 = x.reshape(b * seq, D_MODEL)
        rows_per_cond = seq if mod_rows.stop - mod_rows.start > 1 else b * seq
        new_ret, new_ssd = new_states
        for l in range(DEPTH):
            mv = [mod[l, mod_rows, k][:, None, :] for k in range(N_MOD)]
            sh1, sc1, g1, sh2, sc2, g2 = mv
            proj, dt = _in_proj(x2, rows_per_cond, norm1_w[l].reshape(1, D_MODEL), sc1, sh1,
                                w_in_b, l, w_dt_b[l])
            proj3 = proj.reshape(b, seq, D_MAIN)
            dt3 = dt.reshape(b, seq, NQ * DT_LANES)
            init_r, init_s = (None, None) if states is None else states
            ret_in = proj3 if states is not None else proj3.reshape(1, b * seq, D_MAIN)
            ret = _retention(ret_in, ret_log_decay[l], rope_tabs, init_r, l, new_ret,
                             n_seq=1 if states is not None else b)
            ssd = _ssd(proj3, dt3, d_skip[l], _quad_lanes(dt_bias[l]), _quad_lanes(a_log[l]),
                       conv_w[l], conv_b[l].reshape(1, D_XBC), init_s, l, new_ssd)
            if new_ret is not None:
                new_ret, new_ssd = ret[1], ssd[1]
            x2 = _out_proj(ret[0].reshape(b * seq, D_RET), ssd[0].reshape(b * seq, D_SSD),
                           ssd_norm_w[l].reshape(1, D_SSD), x2, rows_per_cond, g1, w_out_b, l)
            x2 = _ffn(x2, rows_per_cond, norm2_w[l].reshape(1, D_MODEL), sc2, sh2, g2,
                      w_ff1_b, w_ff2_b, l, final_w if l == DEPTH - 1 else None)
        return x2.reshape(b, seq, D_MODEL), new_ret, new_ssd

    empty_states = (jnp.zeros((bp,) + state_ret.shape[1:], F32), jnp.zeros((bp,) + state_ssd.shape[1:], F32))
    y_prompt, new_state_ret, new_state_ssd = run_group(x_prompt, slice(bs, bs + 1), None, None, empty_states)
    y_sample, _, _ = run_group(x_sample, slice(0, bs), rope, (state_ret, state_ssd), (None, None))
    return (y_prompt, y_sample, new_state_ret, new_state_ssd)
```

```python
import functools

import jax
import jax.numpy as jnp
from jax import lax
from jax.experimental import pallas as pl
from jax.experimental.pallas import tpu as pltpu

F32 = jnp.float32
BF16 = jnp.bfloat16

D_MODEL = 2048
DEPTH = 4
CHUNK = 128
H_RET = 8
DK_RET = 128
DV_RET = 128
D_RET = H_RET * DV_RET
H_SSD = 16
P_SSD = 64
D_SSD = H_SSD * P_SSD
G_SSD = 2
N_SSD = 128
D_CONV = 3
D_XBC = D_SSD + 2 * G_SSD * N_SSD
D_FF = 4 * D_MODEL
N_MOD = 6
EPS = 1e-6
ROPE_BASE = 10000.0
GRID_W = 64

D_MAIN = 4 * D_RET + D_SSD + D_XBC
OFF_Q, OFF_K, OFF_V, OFF_G = 0, D_RET, 2 * D_RET, 3 * D_RET
OFF_Z = 4 * D_RET
OFF_XS = OFF_Z + D_SSD
OFF_B = OFF_XS + D_SSD
OFF_C = OFF_B + G_SSD * N_SSD

HQ = 4
NQ = H_SSD // HQ
HP = HQ // 2
DT_LANES = 128
QUAD_LANES = 4 * HQ
COND_ROWS = 16
SCAN_UNROLL = 16
SSD_UNROLL = 2
V7X_VMEM_LIMIT = 56 * 1024 * 1024


def _silu(x):
    return x * jax.nn.sigmoid(x)


def _softplus(x):
    return jnp.maximum(x, 0.0) + jnp.log1p(jnp.exp(-jnp.abs(x)))


def _rms(x, w):
    return x * lax.rsqrt(jnp.mean(x * x, axis=-1, keepdims=True) + EPS) * w


def _params(sem):
    return pltpu.CompilerParams(dimension_semantics=sem, vmem_limit_bytes=V7X_VMEM_LIMIT)


def _ada_kernel(cond_ref, w_ref, b_ref, o_ref):
    s = _silu(cond_ref[...]).astype(BF16)
    o_ref[...] = jnp.dot(s, w_ref[...].astype(BF16), preferred_element_type=F32) + b_ref[...]


def _ada(cond, w_ada, b_ada):
    tn = 1024
    n = N_MOD * D_MODEL
    return pl.pallas_call(
        _ada_kernel,
        grid=(DEPTH, n // tn),
        in_specs=[
            pl.BlockSpec((COND_ROWS, D_MODEL), lambda l, j: (0, 0)),
            pl.BlockSpec((None, D_MODEL, tn), lambda l, j: (l, 0, j)),
            pl.BlockSpec((None, 1, tn), lambda l, j: (l, 0, j)),
        ],
        out_specs=pl.BlockSpec((None, COND_ROWS, tn), lambda l, j: (l, 0, j)),
        out_shape=jax.ShapeDtypeStruct((DEPTH, COND_ROWS, n), F32),
        compiler_params=_params(("arbitrary", "arbitrary")),
        name="ada_mod",
    )(cond, w_ada, b_ada.reshape(DEPTH, 1, n))


def _modulated_norm(x, nw, sc, sh):
    return (_rms(x, nw) * (1.0 + sc) + sh).astype(BF16)


def _in_kernel(x_ref, nw_ref, sc_ref, sh_ref, w_ref, wdt_ref, o_ref, dt_ref, h_cur, h_next, *, nj):
    i, j = pl.program_id(0), pl.program_id(1)
    part = x_ref.shape[0] // nj
    rows = pl.ds(pl.multiple_of(j * part, part), part)

    def stage():
        h_next[rows, :] = _modulated_norm(x_ref[rows, :], nw_ref[...], sc_ref[...], sh_ref[...])

    @pl.when(i == 0)
    def _():
        stage()

    @pl.when(i > 0)
    def _():
        stage()
        dt = jnp.dot(h_cur[rows, :], wdt_ref[...], preferred_element_type=F32)
        for q in range(NQ):
            shift = (DT_LANES - QUAD_LANES * q) % DT_LANES
            dt_ref[rows, q * DT_LANES:(q + 1) * DT_LANES] = pltpu.roll(dt, shift, 1) if shift else dt
        o_ref[...] = jnp.dot(h_cur[...], w_ref[...], preferred_element_type=F32).astype(o_ref.dtype)

    @pl.when(j == nj - 1)
    def _():
        h_cur[...] = h_next[...]


def _in_proj(x2, rows_per_cond, nw, sc, sh, w_in_all, layer, w_dt):
    m = x2.shape[0]
    tm, tn = 512, D_MAIN // 2
    nt = m // tm
    staged = lambda i: jnp.minimum(i, nt - 1)
    done = lambda i: jnp.maximum(i - 1, 0)
    cond_map = lambda i, j: ((staged(i) * tm) // rows_per_cond, 0, 0)
    return pl.pallas_call(
        functools.partial(_in_kernel, nj=D_MAIN // tn),
        grid=(nt + 1, D_MAIN // tn),
        in_specs=[
            pl.BlockSpec((tm, D_MODEL), lambda i, j: (staged(i), 0)),
            pl.BlockSpec((1, D_MODEL), lambda i, j: (0, 0)),
            pl.BlockSpec((None, 1, D_MODEL), cond_map),
            pl.BlockSpec((None, 1, D_MODEL), cond_map),
            pl.BlockSpec((None, D_MODEL, tn), lambda i, j: (layer, 0, jnp.where(i > 0, j, 0))),
            pl.BlockSpec((D_MODEL, DT_LANES), lambda i, j: (0, 0)),
        ],
        out_specs=[
            pl.BlockSpec((tm, tn), lambda i, j: (done(i), jnp.where(i > 0, j, 0))),
            pl.BlockSpec((tm, NQ * DT_LANES), lambda i, j: (done(i), 0)),
        ],
        out_shape=[
            jax.ShapeDtypeStruct((m, D_MAIN), BF16),
            jax.ShapeDtypeStruct((m, NQ * DT_LANES), F32),
        ],
        scratch_shapes=[pltpu.VMEM((tm, D_MODEL), BF16), pltpu.VMEM((tm, D_MODEL), BF16)],
        compiler_params=_params(("arbitrary", "arbitrary")),
        name="in_proj",
    )(x2, nw, sc, sh, w_in_all, w_dt)


def _ret_kernel(*refs, seq, n_seq, rope, has_init, want_state):
    it = iter(refs)
    ld_ref, q_ref, k_ref, v_ref, g_ref = (next(it) for _ in range(5))
    cos_ref = sin_ref = init_ref = st_ref = None
    if rope:
        cos_ref, sin_ref = next(it), next(it)
    if has_init:
        init_ref = next(it)
    if want_state:
        next(it)
    y_ref = next(it)
    if want_state:
        st_ref = next(it)
    q_scr, kt_scr, sf_scr, sb_scr = next(it), next(it), next(it), next(it)

    t = CHUNK
    nc = seq // t
    head = pl.program_id(1)
    la_f = ld_ref[0, head]
    la_b = ld_ref[1, head]

    ii = lax.broadcasted_iota(jnp.int32, (t, t), 0)
    jj = lax.broadcasted_iota(jnp.int32, (t, t), 1)
    dist = (ii - jj).astype(F32)
    lower = ii >= jj
    upper = ii <= jj
    dmat = (jnp.where(lower, jnp.exp(la_f * jnp.where(lower, dist, 0.0)), 0.0)
            + jnp.where(upper, jnp.exp(la_b * jnp.where(upper, -dist, 0.0)), 0.0))
    pos = lax.broadcasted_iota(jnp.int32, (t, 1), 0).astype(F32)
    lane = lax.broadcasted_iota(jnp.int32, (1, t), 1).astype(F32)
    e_f = jnp.exp(la_f * (pos + 1.0))
    e_b = jnp.exp(la_b * (t - pos))
    w_f = jnp.exp(la_f * (t - 1.0 - lane))
    w_b = jnp.exp(la_b * lane)
    dec_f = jnp.exp(jnp.full((1, 1), t, F32) * la_f)
    dec_b = jnp.exp(jnp.full((1, 1), t, F32) * la_b)
    scale = DK_RET ** -0.5
    unroll = min(nc, SCAN_UNROLL)

    def rows(c):
        return pl.ds(pl.multiple_of(c * t, t), t)

    def prep(c, carry):
        r = rows(c)
        q = q_ref[r, :].astype(F32)
        k = k_ref[r, :].astype(F32)
        if rope:
            cs, sn = cos_ref[r, :], sin_ref[r, :]
            q = q * cs + pltpu.roll(q, DK_RET // 2, 1) * sn
            k = k * cs + pltpu.roll(k, DK_RET // 2, 1) * sn
        q_scr[r, :] = (q * scale).astype(BF16)
        kt_scr[c] = k.T.astype(BF16)
        return carry

    lax.fori_loop(0, nc, prep, 0, unroll=unroll)

    def chunk_state(c, w_row):
        kw = (kt_scr[c].astype(F32) * w_row).astype(BF16)
        return jnp.dot(kw, v_ref[rows(c), :], preferred_element_type=F32)

    def states(i, carry, base=0, last=nc - 1):
        s_f, s_b = carry
        c_f, c_b = base + i, base + last - i
        sf_scr[c_f] = s_f.astype(BF16)
        sb_scr[c_b] = s_b.astype(BF16)
        return dec_f * s_f + chunk_state(c_f, w_f), dec_b * s_b + chunk_state(c_b, w_b)

    def outs(c, carry):
        r = rows(c)
        q = q_scr[r, :]
        att = jnp.dot(q, kt_scr[c], preferred_element_type=F32)
        qf = q.astype(F32)
        a = jnp.concatenate([(att * dmat).astype(BF16), (qf * e_f).astype(BF16),
                             (qf * e_b).astype(BF16)], axis=1)
        rhs = jnp.concatenate([v_ref[r, :], sf_scr[c], sb_scr[c]], axis=0)
        y = jnp.dot(a, rhs, preferred_element_type=F32)
        yn = y * lax.rsqrt(jnp.mean(y * y, axis=-1, keepdims=True) + EPS)
        y_ref[r, :] = (_silu(g_ref[r, :].astype(F32)) * yn).astype(y_ref.dtype)
        return carry

    zero = jnp.zeros((DK_RET, DV_RET), F32)
    if n_seq == 1:
        s0_f, s0_b = (init_ref[0].T, init_ref[1].T) if has_init else (zero, zero)
        s_f, s_b = lax.fori_loop(0, nc, states, (s0_f, s0_b), unroll=unroll)
        if want_state:
            st_ref[0, 0] = s_f.T
            st_ref[0, 1] = s_b.T
    else:
        assert not has_init and not rope
        per = nc // n_seq

        def sequence(g, carry):
            s_f = s_b = zero
            for u in range(per):
                s_f, s_b = states(u, (s_f, s_b), base=g * per, last=per - 1)
            if want_state:
                st_ref[g, 0] = s_f.T
                st_ref[g, 1] = s_b.T
            return carry

        lax.fori_loop(0, n_seq, sequence, 0, unroll=min(n_seq, max(1, SCAN_UNROLL // per)))
    lax.fori_loop(0, nc, outs, 0, unroll=unroll)


def _retention(proj3, ret_ld, rope_tabs, init, layer, state_buf, n_seq=1):
    b, seq, _ = proj3.shape
    want_state = state_buf is not None
    aliases = {}
    hblk = lambda off: pl.BlockSpec((None, seq, DK_RET), lambda i, h, o=off // DK_RET: (i, 0, o + h))
    in_specs = [pl.BlockSpec(memory_space=pltpu.SMEM), hblk(OFF_Q), hblk(OFF_K), hblk(OFF_V), hblk(OFF_G)]
    args = [ret_ld, proj3, proj3, proj3, proj3]
    if rope_tabs is not None:
        in_specs += [pl.BlockSpec((seq, DK_RET), lambda i, h: (0, 0))] * 2
        args += list(rope_tabs)
    if init is not None:
        in_specs.append(pl.BlockSpec((None, None, 2, None, DV_RET, DK_RET),
                                     lambda i, h: (i, layer, 0, h, 0, 0)))
        args.append(init)
    out_specs = [pl.BlockSpec((None, seq, DV_RET), lambda i, h: (i, 0, h))]
    out_shape = [jax.ShapeDtypeStruct((b, seq, D_RET), BF16)]
    if want_state:
        aliases[len(args)] = 1
        in_specs.append(pl.BlockSpec(memory_space=pl.ANY))
        args.append(state_buf)
        out_specs.append(pl.BlockSpec((n_seq, None, 2, None, DV_RET, DK_RET),
                                      lambda i, h: (i, layer, 0, h, 0, 0)))
        out_shape.append(jax.ShapeDtypeStruct(state_buf.shape, state_buf.dtype))
    return pl.pallas_call(
        functools.partial(_ret_kernel, seq=seq, n_seq=n_seq, rope=rope_tabs is not None,
                          has_init=init is not None, want_state=want_state),
        grid=(b, H_RET),
        in_specs=in_specs,
        out_specs=out_specs,
        out_shape=out_shape,
        scratch_shapes=[pltpu.VMEM((seq, DK_RET), BF16),
                        pltpu.VMEM((seq // CHUNK, DK_RET, CHUNK), BF16),
                        pltpu.VMEM((seq // CHUNK, DK_RET, DV_RET), BF16),
                        pltpu.VMEM((seq // CHUNK, DK_RET, DV_RET), BF16)],
        input_output_aliases=aliases,
        compiler_params=_params(("arbitrary", "arbitrary")),
        name="retention",
    )(*args)


def _conv_silu_chunk(src_ref, w_ref, b_ref, c, nc, per):
    t = CHUNK
    r0 = pl.multiple_of(c * t, t)
    cur = src_ref[pl.ds(r0, t), :].astype(F32)
    p0 = pl.multiple_of(jnp.maximum(r0 - 16, 0), 16)
    n0 = pl.multiple_of(jnp.minimum(r0 + t, nc * t - 16), 16)
    pos = lax.rem(c, per)
    prev = jnp.where(pos > 0, src_ref[pl.ds(p0, 16), :].astype(F32)[15:16, :], 0.0)
    nxt = jnp.where(pos < per - 1, src_ref[pl.ds(n0, 16), :].astype(F32)[0:1, :], 0.0)
    row = lax.broadcasted_iota(jnp.int32, (t, 1), 0)
    xm1 = jnp.where(row == 0, prev, pltpu.roll(cur, 1, 0))
    xp1 = jnp.where(row == t - 1, nxt, pltpu.roll(cur, t - 1, 0))
    return _silu(xm1 * w_ref[0:1, :] + cur * w_ref[1:2, :] + xp1 * w_ref[2:3, :] + b_ref[...])


def _exact_cumsum(tri_b, x):
    hi = x.astype(BF16)
    r1 = x - hi.astype(F32)
    mid = r1.astype(BF16)
    lo = (r1 - mid.astype(F32)).astype(BF16)
    return (jnp.dot(tri_b, hi, preferred_element_type=F32)
            + jnp.dot(tri_b, mid, preferred_element_type=F32)
            + jnp.dot(tri_b, lo, preferred_element_type=F32))


def _ssd_kernel(*refs, seq, n_seq, has_init, want_state):
    it = iter(refs)
    (dsk_ref, xs_ref, z_ref, b_ref, c_ref, dt_ref, bias_ref, alog_ref,
     wx_ref, bx_ref, wb_ref, bb_ref, wc_ref, bc_ref) = (next(it) for _ in range(14))
    init_ref = st_ref = None
    if has_init:
        init_ref = next(it)
    if want_state:
        next(it)
    u_ref = next(it)
    if want_state:
        st_ref = next(it)
    xc_scr, bt_scr, cc_scr, col_scr, row_scr, sf_scr, sb_scr = (next(it) for _ in range(7))

    t = CHUNK
    nc = seq // t
    per = nc // n_seq
    quad = pl.program_id(1)
    unroll = min(nc, SSD_UNROLL)

    ii = lax.broadcasted_iota(jnp.int32, (t, t), 0)
    jj = lax.broadcasted_iota(jnp.int32, (t, t), 1)
    lower = ii >= jj
    strict_lower = ii > jj
    diag = ii == jj
    tril_b = jnp.where(lower, 1.0, 0.0).astype(BF16)
    lane = lax.broadcasted_iota(jnp.int32, (1, DT_LANES), 1)
    a_neg = -jnp.exp(alog_ref[...])
    dt_bias = bias_ref[...]

    def rows(c):
        return pl.ds(pl.multiple_of(c * t, t), t)

    def prep_group(c, carry):
        bt_scr[c] = _conv_silu_chunk(b_ref, wb_ref, bb_ref, c, nc, per).T.astype(BF16)
        cc_scr[rows(c), :] = _conv_silu_chunk(c_ref, wc_ref, bc_ref, c, nc, per).astype(BF16)
        return carry

    @pl.when(quad % (NQ // G_SSD) == 0)
    def _():
        lax.fori_loop(0, nc, prep_group, 0, unroll=unroll)

    def prep(c, carry):
        r = rows(c)
        xc_scr[r, :] = _conv_silu_chunk(xs_ref, wx_ref, bx_ref, c, nc, per).astype(BF16)
        dt = _softplus(dt_ref[r, :] + dt_bias)
        da = dt * a_neg
        cum = _exact_cumsum(tril_b, da)
        total = cum[t - 1:t, :]
        decay = jnp.where(lane < HQ, cum, total - cum + da)
        packed = jnp.where(lane < 2 * HQ, decay, dt)
        col_scr[r, :] = packed
        row_scr[c] = packed.T[0:4 * HQ, :]
        return carry

    lax.fori_loop(0, nc, prep, 0, unroll=min(nc, 2 * SSD_UNROLL))

    even = lax.broadcasted_iota(jnp.int32, (1, 2 * P_SSD), 1) < P_SSD

    def pair_cols(k):
        return slice(k * 2 * P_SSD, (k + 1) * 2 * P_SSD)

    def chunk_states(c, first, tot_lane):
        rowf = row_scr[c]
        cum_r = rowf[first:first + HQ, :]
        dt_r = rowf[2 * HQ + first:3 * HQ + first, :]
        tot = cum_r[:, tot_lane:tot_lane + 1]
        w = jnp.exp(tot - cum_r) * dt_r
        dec = jnp.exp(tot)
        btf = bt_scr[c].astype(F32)
        r = rows(c)
        decs, css = [], []
        for k in range(HP):
            x_pair = xc_scr[r, pair_cols(k)]
            cs = [jnp.dot((btf * w[hh:hh + 1, :]).astype(BF16), x_pair, preferred_element_type=F32)
                  for hh in (2 * k, 2 * k + 1)]
            css.append(jnp.where(even, cs[0], cs[1]))
            decs.append(jnp.where(even, dec[2 * k:2 * k + 1, :], dec[2 * k + 1:2 * k + 2, :]))
        return decs, css

    def states(i, carry, base=0, last=nc - 1):
        s_f, s_b = carry
        c_f, c_b = base + i, base + last - i
        dec_f, cs_f = chunk_states(c_f, 0, t - 1)
        dec_b, cs_b = chunk_states(c_b, HQ, 0)
        new_f, new_b = [], []
        for k in range(HP):
            sf_scr[c_f * HP + k] = s_f[k].astype(BF16)
            sb_scr[c_b * HP + k] = s_b[k].astype(BF16)
            new_f.append(dec_f[k] * s_f[k] + cs_f[k])
            new_b.append(dec_b[k] * s_b[k] + cs_b[k])
        return tuple(new_f), tuple(new_b)

    def head_lhs(colf, rowf, cb, cf, hh):
        fl, bl = hh, HQ + hh
        rc_f = jnp.broadcast_to(colf[:, fl:fl + 1], (t, t))
        rc_b = jnp.broadcast_to(colf[:, bl:bl + 1], (t, t))
        seg = jnp.where(lower, rc_f - rowf[fl:fl + 1, :], rc_b - rowf[bl:bl + 1, :])
        dt_f = rowf[2 * HQ + fl:2 * HQ + fl + 1, :]
        dt_b = rowf[2 * HQ + bl:2 * HQ + bl + 1, :]
        dt_sel = jnp.where(strict_lower, dt_f, jnp.where(diag, dt_f + dt_b, dt_b))
        return jnp.concatenate([(cb * (jnp.exp(seg) * dt_sel)).astype(BF16),
                                (cf * jnp.exp(rc_f)).astype(BF16),
                                (cf * jnp.exp(rc_b)).astype(BF16)], axis=1)

    def outs(c, carry):
        r = rows(c)
        colf = col_scr[r, :]
        rowf = row_scr[c]
        cmat = cc_scr[r, :]
        cf = cmat.astype(F32)
        cb = jnp.dot(cmat, bt_scr[c], preferred_element_type=F32)
        for k in range(HP):
            x_pair = xc_scr[r, pair_cols(k)]
            rhs = jnp.concatenate([x_pair, sf_scr[c * HP + k], sb_scr[c * HP + k]], axis=0)
            ys = [jnp.dot(head_lhs(colf, rowf, cb, cf, hh), rhs, preferred_element_type=F32)
                  for hh in (2 * k, 2 * k + 1)]
            h0 = quad * HQ + 2 * k
            skip = jnp.where(even, dsk_ref[h0], dsk_ref[h0 + 1])
            y = jnp.where(even, ys[0], ys[1]) + skip * x_pair.astype(F32)
            u_ref[r, pair_cols(k)] = (y * _silu(z_ref[r, pair_cols(k)].astype(F32))).astype(u_ref.dtype)
        return carry

    def pair_state(ref, d, k):
        return ref[d, 2 * k:2 * k + 2].reshape(2 * P_SSD, N_SSD).T

    def put_states(g, s_f, s_b):
        for k in range(HP):
            st_ref[g, 0, 2 * k:2 * k + 2] = s_f[k].T.reshape(2, P_SSD, N_SSD)
            st_ref[g, 1, 2 * k:2 * k + 2] = s_b[k].T.reshape(2, P_SSD, N_SSD)

    zero = tuple(jnp.zeros((N_SSD, 2 * P_SSD), F32) for _ in range(HP))
    if n_seq == 1:
        s0_f, s0_b = zero, zero
        if has_init:
            s0_f = tuple(pair_state(init_ref, 0, k) for k in range(HP))
            s0_b = tuple(pair_state(init_ref, 1, k) for k in range(HP))
        s_f, s_b = lax.fori_loop(0, nc, states, (s0_f, s0_b), unroll=min(nc, 2 * SSD_UNROLL))
        if want_state:
            put_states(0, s_f, s_b)
    else:
        assert not has_init

        def sequence(g, carry):
            s_f = s_b = zero
            for u in range(per):
                s_f, s_b = states(u, (s_f, s_b), base=g * per, last=per - 1)
            if want_state:
                put_states(g, s_f, s_b)
            return carry

        lax.fori_loop(0, n_seq, sequence, 0, unroll=min(n_seq, max(1, 2 * SSD_UNROLL // per)))
    lax.fori_loop(0, nc, outs, 0, unroll=min(nc, 2 * SSD_UNROLL))


def _ssd(proj3, dt3, d_skip, dt_bias_q, a_log_q, conv_w, conv_b, init, layer, state_buf, n_seq=1):
    b, seq, _ = proj3.shape
    want_state = state_buf is not None
    aliases = {}
    wq = HQ * P_SSD
    col = lambda width, off, fn: pl.BlockSpec((None, seq, width),
                                              lambda i, q, o=off // width: (i, 0, o + fn(q)))
    ident = lambda q: q
    group = lambda q: q // (NQ // G_SSD)
    xoff = 0
    boff = D_SSD
    coff = D_SSD + G_SSD * N_SSD
    cw = lambda width, off, fn: pl.BlockSpec((D_CONV, width), lambda i, q, o=off // width: (0, o + fn(q)))
    cbias = lambda width, off, fn: pl.BlockSpec((1, width), lambda i, q, o=off // width: (0, o + fn(q)))
    in_specs = [
        pl.BlockSpec(memory_space=pltpu.SMEM),
        col(wq, OFF_XS, ident), col(wq, OFF_Z, ident), col(N_SSD, OFF_B, group), col(N_SSD, OFF_C, group),
        pl.BlockSpec((None, seq, DT_LANES), lambda i, q: (i, 0, q)),
        pl.BlockSpec((None, 1, DT_LANES), lambda i, q: (q, 0, 0)),
        pl.BlockSpec((None, 1, DT_LANES), lambda i, q: (q, 0, 0)),
        cw(wq, xoff, ident), cbias(wq, xoff, ident),
        cw(N_SSD, boff, group), cbias(N_SSD, boff, group),
        cw(N_SSD, coff, group), cbias(N_SSD, coff, group),
    ]
    args = [d_skip, proj3, proj3, proj3, proj3, dt3, dt_bias_q, a_log_q,
            conv_w, conv_b, conv_w, conv_b, conv_w, conv_b]
    st_spec = pl.BlockSpec((None, None, 2, HQ, P_SSD, N_SSD), lambda i, q: (i, layer, 0, q, 0, 0))
    if init is not None:
        in_specs.append(st_spec)
        args.append(init)
    out_specs = [pl.BlockSpec((None, seq, wq), lambda i, q: (i, 0, q))]
    out_shape = [jax.ShapeDtypeStruct((b, seq, D_SSD), BF16)]
    if want_state:
        aliases[len(args)] = 1
        in_specs.append(pl.BlockSpec(memory_space=pl.ANY))
        args.append(state_buf)
        out_specs.append(pl.BlockSpec((n_seq, None, 2, HQ, P_SSD, N_SSD),
                                      lambda i, q: (i, layer, 0, q, 0, 0)))
        out_shape.append(jax.ShapeDtypeStruct(state_buf.shape, state_buf.dtype))
    nc = seq // CHUNK
    return pl.pallas_call(
        functools.partial(_ssd_kernel, seq=seq, n_seq=n_seq, has_init=init is not None,
                          want_state=want_state),
        grid=(b, NQ),
        in_specs=in_specs,
        out_specs=out_specs,
        out_shape=out_shape,
        scratch_shapes=[
            pltpu.VMEM((seq, wq), BF16), pltpu.VMEM((nc, N_SSD, CHUNK), BF16), pltpu.VMEM((seq, N_SSD), BF16),
            pltpu.VMEM((seq, DT_LANES), F32), pltpu.VMEM((nc, 4 * HQ, CHUNK), F32),
            pltpu.VMEM((nc * HP, N_SSD, 2 * P_SSD), BF16), pltpu.VMEM((nc * HP, N_SSD, 2 * P_SSD), BF16),
        ],
        input_output_aliases=aliases,
        compiler_params=_params(("arbitrary", "arbitrary")),
        name="ssd",
    )(*args)


def _out_kernel(yr_ref, us_ref, nw_ref, x_ref, g_ref, w_ref, o_ref):
    a = jnp.concatenate([yr_ref[...], _rms(us_ref[...].astype(F32), nw_ref[...]).astype(BF16)], axis=1)
    o_ref[...] = x_ref[...] + g_ref[...] * jnp.dot(a, w_ref[...], preferred_element_type=F32)


def _out_proj(yr, us, ssd_nw, x2, rows_per_cond, gate, w_out_all, layer):
    m = x2.shape[0]
    tm = 512
    return pl.pallas_call(
        _out_kernel,
        grid=(m // tm,),
        in_specs=[
            pl.BlockSpec((tm, D_RET), lambda i: (i, 0)),
            pl.BlockSpec((tm, D_SSD), lambda i: (i, 0)),
            pl.BlockSpec((1, D_SSD), lambda i: (0, 0)),
            pl.BlockSpec((tm, D_MODEL), lambda i: (i, 0)),
            pl.BlockSpec((None, 1, D_MODEL), lambda i: ((i * tm) // rows_per_cond, 0, 0)),
            pl.BlockSpec((None, D_RET + D_SSD, D_MODEL), lambda i: (layer, 0, 0)),
        ],
        out_specs=pl.BlockSpec((tm, D_MODEL), lambda i: (i, 0)),
        out_shape=jax.ShapeDtypeStruct((m, D_MODEL), F32),
        compiler_params=_params(("arbitrary",)),
        name="out_proj",
    )(yr, us, ssd_nw, x2, gate, w_out_all)


def _ffn_kernel(*refs, final, nf):
    it = iter(refs)
    x_ref, res_ref, nw_ref, sc_ref, sh_ref, g_ref, w1_ref, w2_ref = (next(it) for _ in range(8))
    fw_ref = next(it) if final else None
    o_ref, h_cur, h_next, acc_scr = next(it), next(it), next(it), next(it)
    i, f = pl.program_id(0), pl.program_id(1)
    part = x_ref.shape[0] // nf
    rows = pl.ds(pl.multiple_of(f * part, part), part)

    def stage():
        h_next[rows, :] = _modulated_norm(x_ref[rows, :], nw_ref[...], sc_ref[...], sh_ref[...])

    def step(first):
        stage()
        a = jnp.maximum(jnp.dot(h_cur[...], w1_ref[...], preferred_element_type=F32), 0.0)
        d = jnp.dot((a * a).astype(BF16), w2_ref[...], preferred_element_type=F32)
        if first:
            acc_scr[...] = d
        else:
            acc_scr[...] += d

    pl.when(i == 0)(stage)
    pl.when((i > 0) & (f == 0))(functools.partial(step, True))
    pl.when((i > 0) & (f > 0))(functools.partial(step, False))

    @pl.when((i > 0) & (f == nf - 1))
    def _():
        y = res_ref[...] + g_ref[...] * acc_scr[...]
        if final:
            y = _rms(y, fw_ref[...])
        o_ref[...] = y

    @pl.when(f == nf - 1)
    def _():
        h_cur[...] = h_next[...]


def _ffn(x2, rows_per_cond, nw, sc, sh, gate, w1, w2, layer, final_w):
    m = x2.shape[0]
    tm, tf = 512, 1024
    nt, nf = m // tm, D_FF // tf
    staged = lambda i: jnp.minimum(i, nt - 1)
    done = lambda i: jnp.maximum(i - 1, 0)
    wtile = lambda i, f: jnp.where(i > 0, f, 0)
    vec = pl.BlockSpec((1, D_MODEL), lambda i, f: (0, 0))
    cvec = lambda tile: pl.BlockSpec((None, 1, D_MODEL),
                                     lambda i, f: ((tile(i) * tm) // rows_per_cond, 0, 0))
    in_specs = [pl.BlockSpec((tm, D_MODEL), lambda i, f: (staged(i), 0)),
                pl.BlockSpec((tm, D_MODEL), lambda i, f: (done(i), 0)),
                vec, cvec(staged), cvec(staged), cvec(done),
                pl.BlockSpec((None, D_MODEL, tf), lambda i, f: (layer, 0, wtile(i, f))),
                pl.BlockSpec((None, tf, D_MODEL), lambda i, f: (layer, wtile(i, f), 0))]
    args = [x2, x2, nw, sc, sh, gate, w1, w2]
    if final_w is not None:
        in_specs.append(vec)
        args.append(final_w)
    return pl.pallas_call(
        functools.partial(_ffn_kernel, final=final_w is not None, nf=nf),
        grid=(nt + 1, nf),
        in_specs=in_specs,
        out_specs=pl.BlockSpec((tm, D_MODEL), lambda i, f: (done(i), 0)),
        out_shape=jax.ShapeDtypeStruct((m, D_MODEL), F32),
        scratch_shapes=[pltpu.VMEM((tm, D_MODEL), BF16), pltpu.VMEM((tm, D_MODEL), BF16),
                        pltpu.VMEM((tm, D_MODEL), F32)],
        compiler_params=_params(("arbitrary", "arbitrary")),
        name="ffn",
    )(*args)


def _rope_tables(seq):
    pos = jnp.arange(seq)
    row = (pos // GRID_W).astype(F32)
    col = (pos % GRID_W).astype(F32)
    half = DK_RET // 2
    inv = 1.0 / (ROPE_BASE ** (jnp.arange(0, half, 2, dtype=F32) / half))
    ang = jnp.concatenate([row[:, None] * inv, col[:, None] * inv], -1)
    cs, sn = jnp.cos(ang), jnp.sin(ang)
    return jnp.concatenate([cs, cs], -1), jnp.concatenate([-sn, sn], -1)


def _quad_lanes(v):
    pad = jnp.zeros((DT_LANES - 4 * HQ,), v.dtype)
    quads = []
    for q in range(NQ):
        fb = [v[0, q * HQ:(q + 1) * HQ], v[1, q * HQ:(q + 1) * HQ]]
        quads.append(jnp.concatenate(fb + fb + [pad]))
    return jnp.stack(quads)[:, None, :]


def _dt_weight(w_dt):
    parts = []
    for q in range(NQ):
        fb = [w_dt[:, q * HQ:(q + 1) * HQ], w_dt[:, H_SSD + q * HQ:H_SSD + (q + 1) * HQ]]
        parts += fb + fb
    parts.append(jnp.zeros((D_MODEL, DT_LANES - NQ * QUAD_LANES), w_dt.dtype))
    return jnp.concatenate(parts, axis=1)


def kernel(x_prompt, x_sample, state_ret, state_ssd, c, c_ctx, w_ada, b_ada, norm1_w, w_in,
           ret_log_decay, conv_w, conv_b, dt_bias, a_log, d_skip, ssd_norm_w, w_out, norm2_w,
           w_ff1, w_ff2, final_norm_w):
    bp, sp, _ = x_prompt.shape
    bs, ss, _ = x_sample.shape

    cond = jnp.zeros((COND_ROWS, D_MODEL), F32).at[:bs].set(c).at[bs].set(c_ctx)
    mod = _ada(cond, w_ada, b_ada).reshape(DEPTH, COND_ROWS, N_MOD, D_MODEL)

    rope = _rope_tables(ss)
    final_w = final_norm_w.reshape(1, D_MODEL)
    w_in_b = w_in.astype(BF16)
    w_dt_b = [_dt_weight(w_in[l, :, D_MAIN:]).astype(BF16) for l in range(DEPTH)]
    w_out_b = w_out.astype(BF16)
    w_ff1_b = w_ff1.astype(BF16)
    w_ff2_b = w_ff2.astype(BF16)

    def run_group(x, mod_rows, rope_tabs, states, new_states):
        b, seq, _ = x.shape
        x2 = x.reshape(b * seq, D_MODEL)
        rows_per_cond = seq if mod_rows.stop - mod_rows.start > 1 else b * seq
        new_ret, new_ssd = new_states
        for l in range(DEPTH):
            mv = [mod[l, mod_rows, k][:, None, :] for k in range(N_MOD)]
            sh1, sc1, g1, sh2, sc2, g2 = mv
            proj, dt = _in_proj(x2, rows_per_cond, norm1_w[l].reshape(1, D_MODEL), sc1, sh1,
                                w_in_b, l, w_dt_b[l])
            init_r, init_s = (None, None) if states is None else states
            blocks, n_seq = (b, 1) if states is not None else (1, b)
            proj3 = proj.reshape(blocks, n_seq * seq, D_MAIN)
            dt3 = dt.reshape(blocks, n_seq * seq, NQ * DT_LANES)
            ret = _retention(proj3, ret_log_decay[l], rope_tabs, init_r, l, new_ret, n_seq=n_seq)
            ssd = _ssd(proj3, dt3, d_skip[l], _quad_lanes(dt_bias[l]), _quad_lanes(a_log[l]),
                       conv_w[l], conv_b[l].reshape(1, D_XBC), init_s, l, new_ssd, n_seq=n_seq)
            if new_ret is not None:
                new_ret, new_ssd = ret[1], ssd[1]
            x2 = _out_proj(ret[0].reshape(b * seq, D_RET), ssd[0].reshape(b * seq, D_SSD),
                           ssd_norm_w[l].reshape(1, D_SSD), x2, rows_per_cond, g1, w_out_b, l)
            x2 = _ffn(x2, rows_per_cond, norm2_w[l].reshape(1, D_MODEL), sc2, sh2, g2,
                      w_ff1_b, w_ff2_b, l, final_w if l == DEPTH - 1 else None)
        return x2.reshape(b, seq, D_MODEL), new_ret, new_ssd

    empty_states = (jnp.zeros((bp,) + state_ret.shape[1:], F32), jnp.zeros((bp,) + state_ssd.shape[1:], F32))
    y_prompt, new_state_ret, new_state_ssd = run_group(x_prompt, slice(bs, bs + 1), None, None, empty_states)
    y_sample, _, _ = run_group(x_sample, slice(0, bs), rope, (state_ret, state_ssd), (None, None))
    return (y_prompt, y_sample, new_state_ret, new_state_ssd)
```

```python
import functools

import jax
import jax.numpy as jnp
from jax import lax
from jax.experimental import pallas as pl
from jax.experimental.pallas import tpu as pltpu

F32 = jnp.float32
BF16 = jnp.bfloat16

D_MODEL = 2048
DEPTH = 4
CHUNK = 128
H_RET = 8
DK_RET = 128
DV_RET = 128
D_RET = H_RET * DV_RET
H_SSD = 16
P_SSD = 64
D_SSD = H_SSD * P_SSD
G_SSD = 2
N_SSD = 128
D_CONV = 3
D_XBC = D_SSD + 2 * G_SSD * N_SSD
D_FF = 4 * D_MODEL
N_MOD = 6
EPS = 1e-6
ROPE_BASE = 10000.0
GRID_W = 64

D_MAIN = 4 * D_RET + D_SSD + D_XBC
OFF_Q, OFF_K, OFF_V, OFF_G = 0, D_RET, 2 * D_RET, 3 * D_RET
OFF_Z = 4 * D_RET
OFF_XS = OFF_Z + D_SSD
OFF_B = OFF_XS + D_SSD
OFF_C = OFF_B + G_SSD * N_SSD

HQ = 4
NQ = H_SSD // HQ
HP = HQ // 2
DT_LANES = 128
QUAD_LANES = 4 * HQ
COND_ROWS = 16
SCAN_UNROLL = 32
SSD_UNROLL = 2
V7X_VMEM_LIMIT = 56 * 1024 * 1024


def _silu(x):
    return x * jax.nn.sigmoid(x)


def _softplus(x):
    return jnp.maximum(x, 0.0) + jnp.log1p(jnp.exp(-jnp.abs(x)))


def _rms(x, w):
    return x * lax.rsqrt(jnp.mean(x * x, axis=-1, keepdims=True) + EPS) * w


def _params(sem):
    return pltpu.CompilerParams(dimension_semantics=sem, vmem_limit_bytes=V7X_VMEM_LIMIT)


def _ada_kernel(cond_ref, w_ref, b_ref, o_ref):
    s = _silu(cond_ref[...]).astype(BF16)
    o_ref[...] = jnp.dot(s, w_ref[...].astype(BF16), preferred_element_type=F32) + b_ref[...]


def _ada(cond, w_ada, b_ada):
    tn = 2048
    n = N_MOD * D_MODEL
    return pl.pallas_call(
        _ada_kernel,
        grid=(DEPTH, n // tn),
        in_specs=[
            pl.BlockSpec((COND_ROWS, D_MODEL), lambda l, j: (0, 0)),
            pl.BlockSpec((None, D_MODEL, tn), lambda l, j: (l, 0, j)),
            pl.BlockSpec((None, 1, tn), lambda l, j: (l, 0, j)),
        ],
        out_specs=pl.BlockSpec((None, COND_ROWS, tn), lambda l, j: (l, 0, j)),
        out_shape=jax.ShapeDtypeStruct((DEPTH, COND_ROWS, n), F32),
        compiler_params=_params(("arbitrary", "arbitrary")),
        name="ada_mod",
    )(cond, w_ada, b_ada.reshape(DEPTH, 1, n))


def _modulated_norm(x, nw, sc, sh):
    return (_rms(x, nw) * (1.0 + sc) + sh).astype(BF16)


def _in_kernel(x_ref, nw_ref, sc_ref, sh_ref, w_ref, wdt_ref, o_ref, dt_ref, h_cur, h_next, *, nj):
    i, j = pl.program_id(0), pl.program_id(1)
    part = x_ref.shape[0] // nj
    rows = pl.ds(pl.multiple_of(j * part, part), part)

    def stage():
        h_next[rows, :] = _modulated_norm(x_ref[rows, :], nw_ref[...], sc_ref[...], sh_ref[...])

    @pl.when(i == 0)
    def _():
        stage()

    @pl.when(i > 0)
    def _():
        stage()
        dt = jnp.dot(h_cur[rows, :], wdt_ref[...], preferred_element_type=F32)
        for q in range(NQ):
            shift = (DT_LANES - QUAD_LANES * q) % DT_LANES
            dt_ref[rows, q * DT_LANES:(q + 1) * DT_LANES] = pltpu.roll(dt, shift, 1) if shift else dt
        o_ref[...] = jnp.dot(h_cur[...], w_ref[...], preferred_element_type=F32).astype(o_ref.dtype)

    @pl.when(j == nj - 1)
    def _():
        h_cur[...] = h_next[...]


def _in_proj(x2, rows_per_cond, nw, sc, sh, w_in_all, layer, w_dt):
    m = x2.shape[0]
    tm, tn = 512, D_MAIN // 2
    nt = m // tm
    staged = lambda i: jnp.minimum(i, nt - 1)
    done = lambda i: jnp.maximum(i - 1, 0)
    cond_map = lambda i, j: ((staged(i) * tm) // rows_per_cond, 0, 0)
    return pl.pallas_call(
        functools.partial(_in_kernel, nj=D_MAIN // tn),
        grid=(nt + 1, D_MAIN // tn),
        in_specs=[
            pl.BlockSpec((tm, D_MODEL), lambda i, j: (staged(i), 0)),
            pl.BlockSpec((1, D_MODEL), lambda i, j: (0, 0)),
            pl.BlockSpec((None, 1, D_MODEL), cond_map),
            pl.BlockSpec((None, 1, D_MODEL), cond_map),
            pl.BlockSpec((None, D_MODEL, tn), lambda i, j: (layer, 0, jnp.where(i > 0, j, 0))),
            pl.BlockSpec((D_MODEL, DT_LANES), lambda i, j: (0, 0)),
        ],
        out_specs=[
            pl.BlockSpec((tm, tn), lambda i, j: (done(i), jnp.where(i > 0, j, 0))),
            pl.BlockSpec((tm, NQ * DT_LANES), lambda i, j: (done(i), 0)),
        ],
        out_shape=[
            jax.ShapeDtypeStruct((m, D_MAIN), BF16),
            jax.ShapeDtypeStruct((m, NQ * DT_LANES), F32),
        ],
        scratch_shapes=[pltpu.VMEM((tm, D_MODEL), BF16), pltpu.VMEM((tm, D_MODEL), BF16)],
        compiler_params=_params(("arbitrary", "arbitrary")),
        name="in_proj",
    )(x2, nw, sc, sh, w_in_all, w_dt)


def _ret_kernel(*refs, seq, n_seq, rope, has_init, want_state):
    it = iter(refs)
    ld_ref, q_ref, k_ref, v_ref, g_ref = (next(it) for _ in range(5))
    cos_ref = sin_ref = init_ref = st_ref = None
    if rope:
        cos_ref, sin_ref = next(it), next(it)
    if has_init:
        init_ref = next(it)
    if want_state:
        next(it)
    y_ref = next(it)
    if want_state:
        st_ref = next(it)
    q_scr, kt_scr, sf_scr, sb_scr = next(it), next(it), next(it), next(it)

    t = CHUNK
    nc = seq // t
    head = pl.program_id(1)
    la_f = ld_ref[0, head]
    la_b = ld_ref[1, head]

    ii = lax.broadcasted_iota(jnp.int32, (t, t), 0)
    jj = lax.broadcasted_iota(jnp.int32, (t, t), 1)
    dist = (ii - jj).astype(F32)
    lower = ii >= jj
    upper = ii <= jj
    dmat = (jnp.where(lower, jnp.exp(la_f * jnp.where(lower, dist, 0.0)), 0.0)
            + jnp.where(upper, jnp.exp(la_b * jnp.where(upper, -dist, 0.0)), 0.0))
    pos = lax.broadcasted_iota(jnp.int32, (t, 1), 0).astype(F32)
    lane = lax.broadcasted_iota(jnp.int32, (1, t), 1).astype(F32)
    e_f = jnp.exp(la_f * (pos + 1.0))
    e_b = jnp.exp(la_b * (t - pos))
    w_f = jnp.exp(la_f * (t - 1.0 - lane))
    w_b = jnp.exp(la_b * lane)
    dec_f = jnp.exp(jnp.full((1, 1), t, F32) * la_f)
    dec_b = jnp.exp(jnp.full((1, 1), t, F32) * la_b)
    scale = DK_RET ** -0.5
    unroll = min(nc, SCAN_UNROLL)

    def rows(c):
        return pl.ds(pl.multiple_of(c * t, t), t)

    def prep(c, carry):
        r = rows(c)
        q = q_ref[r, :].astype(F32)
        k = k_ref[r, :].astype(F32)
        if rope:
            cs, sn = cos_ref[r, :], sin_ref[r, :]
            q = q * cs + pltpu.roll(q, DK_RET // 2, 1) * sn
            k = k * cs + pltpu.roll(k, DK_RET // 2, 1) * sn
        q_scr[r, :] = (q * scale).astype(BF16)
        kt_scr[c] = k.T.astype(BF16)
        return carry

    lax.fori_loop(0, nc, prep, 0, unroll=unroll)

    def chunk_state(c, w_row):
        kw = (kt_scr[c].astype(F32) * w_row).astype(BF16)
        return jnp.dot(kw, v_ref[rows(c), :], preferred_element_type=F32)

    def states(i, carry, base=0, last=nc - 1):
        s_f, s_b = carry
        c_f, c_b = base + i, base + last - i
        sf_scr[c_f] = s_f.astype(BF16)
        sb_scr[c_b] = s_b.astype(BF16)
        return dec_f * s_f + chunk_state(c_f, w_f), dec_b * s_b + chunk_state(c_b, w_b)

    def outs(c, carry):
        r = rows(c)
        q = q_scr[r, :]
        att = jnp.dot(q, kt_scr[c], preferred_element_type=F32)
        qf = q.astype(F32)
        a = jnp.concatenate([(att * dmat).astype(BF16), (qf * e_f).astype(BF16),
                             (qf * e_b).astype(BF16)], axis=1)
        rhs = jnp.concatenate([v_ref[r, :], sf_scr[c], sb_scr[c]], axis=0)
        y = jnp.dot(a, rhs, preferred_element_type=F32)
        yn = y * lax.rsqrt(jnp.mean(y * y, axis=-1, keepdims=True) + EPS)
        y_ref[r, :] = (_silu(g_ref[r, :].astype(F32)) * yn).astype(y_ref.dtype)
        return carry

    zero = jnp.zeros((DK_RET, DV_RET), F32)
    if n_seq == 1:
        s0_f, s0_b = (init_ref[0].T, init_ref[1].T) if has_init else (zero, zero)
        s_f, s_b = lax.fori_loop(0, nc, states, (s0_f, s0_b), unroll=unroll)
        if want_state:
            st_ref[0, 0] = s_f.T
            st_ref[0, 1] = s_b.T
    else:
        assert not has_init and not rope
        per = nc // n_seq

        def sequence(g, carry):
            s_f = s_b = zero
            for u in range(per):
                s_f, s_b = states(u, (s_f, s_b), base=g * per, last=per - 1)
            if want_state:
                st_ref[g, 0] = s_f.T
                st_ref[g, 1] = s_b.T
            return carry

        lax.fori_loop(0, n_seq, sequence, 0, unroll=min(n_seq, max(1, SCAN_UNROLL // per)))
    lax.fori_loop(0, nc, outs, 0, unroll=unroll)


def _retention(proj3, ret_ld, rope_tabs, init, layer, state_buf, n_seq=1):
    b, seq, _ = proj3.shape
    want_state = state_buf is not None
    aliases = {}
    hblk = lambda off: pl.BlockSpec((None, seq, DK_RET), lambda i, h, o=off // DK_RET: (i, 0, o + h))
    in_specs = [pl.BlockSpec(memory_space=pltpu.SMEM), hblk(OFF_Q), hblk(OFF_K), hblk(OFF_V), hblk(OFF_G)]
    args = [ret_ld, proj3, proj3, proj3, proj3]
    if rope_tabs is not None:
        in_specs += [pl.BlockSpec((seq, DK_RET), lambda i, h: (0, 0))] * 2
        args += list(rope_tabs)
    if init is not None:
        in_specs.append(pl.BlockSpec((None, None, 2, None, DV_RET, DK_RET),
                                     lambda i, h: (i, layer, 0, h, 0, 0)))
        args.append(init)
    out_specs = [pl.BlockSpec((None, seq, DV_RET), lambda i, h: (i, 0, h))]
    out_shape = [jax.ShapeDtypeStruct((b, seq, D_RET), BF16)]
    if want_state:
        aliases[len(args)] = 1
        in_specs.append(pl.BlockSpec(memory_space=pl.ANY))
        args.append(state_buf)
        out_specs.append(pl.BlockSpec((n_seq, None, 2, None, DV_RET, DK_RET),
                                      lambda i, h: (i, layer, 0, h, 0, 0)))
        out_shape.append(jax.ShapeDtypeStruct(state_buf.shape, state_buf.dtype))
    return pl.pallas_call(
        functools.partial(_ret_kernel, seq=seq, n_seq=n_seq, rope=rope_tabs is not None,
                          has_init=init is not None, want_state=want_state),
        grid=(b, H_RET),
        in_specs=in_specs,
        out_specs=out_specs,
        out_shape=out_shape,
        scratch_shapes=[pltpu.VMEM((seq, DK_RET), BF16),
                        pltpu.VMEM((seq // CHUNK, DK_RET, CHUNK), BF16),
                        pltpu.VMEM((seq // CHUNK, DK_RET, DV_RET), BF16),
                        pltpu.VMEM((seq // CHUNK, DK_RET, DV_RET), BF16)],
        input_output_aliases=aliases,
        compiler_params=_params(("arbitrary", "arbitrary")),
        name="retention",
    )(*args)


def _conv_silu_chunk(src_ref, w_ref, b_ref, c, nc, per):
    t = CHUNK
    r0 = pl.multiple_of(c * t, t)
    cur = src_ref[pl.ds(r0, t), :].astype(F32)
    p0 = pl.multiple_of(jnp.maximum(r0 - 16, 0), 16)
    n0 = pl.multiple_of(jnp.minimum(r0 + t, nc * t - 16), 16)
    pos = lax.rem(c, per)
    prev = jnp.where(pos > 0, src_ref[pl.ds(p0, 16), :].astype(F32)[15:16, :], 0.0)
    nxt = jnp.where(pos < per - 1, src_ref[pl.ds(n0, 16), :].astype(F32)[0:1, :], 0.0)
    row = lax.broadcasted_iota(jnp.int32, (t, 1), 0)
    xm1 = jnp.where(row == 0, prev, pltpu.roll(cur, 1, 0))
    xp1 = jnp.where(row == t - 1, nxt, pltpu.roll(cur, t - 1, 0))
    return _silu(xm1 * w_ref[0:1, :] + cur * w_ref[1:2, :] + xp1 * w_ref[2:3, :] + b_ref[...])


def _exact_cumsum(tri_b, x):
    hi = x.astype(BF16)
    r1 = x - hi.astype(F32)
    mid = r1.astype(BF16)
    lo = (r1 - mid.astype(F32)).astype(BF16)
    return (jnp.dot(tri_b, hi, preferred_element_type=F32)
            + jnp.dot(tri_b, mid, preferred_element_type=F32)
            + jnp.dot(tri_b, lo, preferred_element_type=F32))


def _ssd_kernel(*refs, seq, n_seq, has_init, want_state):
    it = iter(refs)
    (dsk_ref, xs_ref, z_ref, b_ref, c_ref, dt_ref, bias_ref, alog_ref,
     wx_ref, bx_ref, wb_ref, bb_ref, wc_ref, bc_ref) = (next(it) for _ in range(14))
    init_ref = st_ref = None
    if has_init:
        init_ref = next(it)
    if want_state:
        next(it)
    u_ref = next(it)
    if want_state:
        st_ref = next(it)
    xc_scr, bt_scr, cc_scr, col_scr, row_scr, sf_scr, sb_scr = (next(it) for _ in range(7))

    t = CHUNK
    nc = seq // t
    per = nc // n_seq
    quad = pl.program_id(1)
    unroll = min(nc, SSD_UNROLL)

    ii = lax.broadcasted_iota(jnp.int32, (t, t), 0)
    jj = lax.broadcasted_iota(jnp.int32, (t, t), 1)
    lower = ii >= jj
    strict_lower = ii > jj
    diag = ii == jj
    tril_b = jnp.where(lower, 1.0, 0.0).astype(BF16)
    lane = lax.broadcasted_iota(jnp.int32, (1, DT_LANES), 1)
    a_neg = -jnp.exp(alog_ref[...])
    dt_bias = bias_ref[...]

    def rows(c):
        return pl.ds(pl.multiple_of(c * t, t), t)

    def prep_group(c, carry):
        bt_scr[c] = _conv_silu_chunk(b_ref, wb_ref, bb_ref, c, nc, per).T.astype(BF16)
        cc_scr[rows(c), :] = _conv_silu_chunk(c_ref, wc_ref, bc_ref, c, nc, per).astype(BF16)
        return carry

    @pl.when(quad % (NQ // G_SSD) == 0)
    def _():
        lax.fori_loop(0, nc, prep_group, 0, unroll=unroll)

    def prep(c, carry):
        r = rows(c)
        xc_scr[r, :] = _conv_silu_chunk(xs_ref, wx_ref, bx_ref, c, nc, per).astype(BF16)
        dt = _softplus(dt_ref[r, :] + dt_bias)
        da = dt * a_neg
        cum = _exact_cumsum(tril_b, da)
        total = cum[t - 1:t, :]
        decay = jnp.where(lane < HQ, cum, total - cum + da)
        packed = jnp.where(lane < 2 * HQ, decay, dt)
        col_scr[r, :] = packed
        row_scr[c] = packed.T[0:4 * HQ, :]
        return carry

    lax.fori_loop(0, nc, prep, 0, unroll=min(nc, 2 * SSD_UNROLL))

    even = lax.broadcasted_iota(jnp.int32, (1, 2 * P_SSD), 1) < P_SSD

    def pair_cols(k):
        return slice(k * 2 * P_SSD, (k + 1) * 2 * P_SSD)

    def chunk_states(c, first, tot_lane):
        rowf = row_scr[c]
        cum_r = rowf[first:first + HQ, :]
        dt_r = rowf[2 * HQ + first:3 * HQ + first, :]
        tot = cum_r[:, tot_lane:tot_lane + 1]
        w = jnp.exp(tot - cum_r) * dt_r
        dec = jnp.exp(tot)
        btf = bt_scr[c].astype(F32)
        r = rows(c)
        decs, css = [], []
        for k in range(HP):
            x_pair = xc_scr[r, pair_cols(k)]
            cs = [jnp.dot((btf * w[hh:hh + 1, :]).astype(BF16), x_pair, preferred_element_type=F32)
                  for hh in (2 * k, 2 * k + 1)]
            css.append(jnp.where(even, cs[0], cs[1]))
            decs.append(jnp.where(even, dec[2 * k:2 * k + 1, :], dec[2 * k + 1:2 * k + 2, :]))
        return decs, css

    def states(i, carry, base=0, last=nc - 1):
        s_f, s_b = carry
        c_f, c_b = base + i, base + last - i
        dec_f, cs_f = chunk_states(c_f, 0, t - 1)
        dec_b, cs_b = chunk_states(c_b, HQ, 0)
        new_f, new_b = [], []
        for k in range(HP):
            sf_scr[c_f * HP + k] = s_f[k].astype(BF16)
            sb_scr[c_b * HP + k] = s_b[k].astype(BF16)
            new_f.append(dec_f[k] * s_f[k] + cs_f[k])
            new_b.append(dec_b[k] * s_b[k] + cs_b[k])
        return tuple(new_f), tuple(new_b)

    def head_lhs(colf, rowf, cb, cf, hh):
        fl, bl = hh, HQ + hh
        rc_f = jnp.broadcast_to(colf[:, fl:fl + 1], (t, t))
        rc_b = jnp.broadcast_to(colf[:, bl:bl + 1], (t, t))
        seg = jnp.where(lower, rc_f - rowf[fl:fl + 1, :], rc_b - rowf[bl:bl + 1, :])
        dt_f = rowf[2 * HQ + fl:2 * HQ + fl + 1, :]
        dt_b = rowf[2 * HQ + bl:2 * HQ + bl + 1, :]
        dt_sel = jnp.where(strict_lower, dt_f, jnp.where(diag, dt_f + dt_b, dt_b))
        return jnp.concatenate([(cb * (jnp.exp(seg) * dt_sel)).astype(BF16),
                                (cf * jnp.exp(rc_f)).astype(BF16),
                                (cf * jnp.exp(rc_b)).astype(BF16)], axis=1)

    def outs(c, carry):
        r = rows(c)
        colf = col_scr[r, :]
        rowf = row_scr[c]
        cmat = cc_scr[r, :]
        cf = cmat.astype(F32)
        cb = jnp.dot(cmat, bt_scr[c], preferred_element_type=F32)
        for k in range(HP):
            x_pair = xc_scr[r, pair_cols(k)]
            rhs = jnp.concatenate([x_pair, sf_scr[c * HP + k], sb_scr[c * HP + k]], axis=0)
            ys = [jnp.dot(head_lhs(colf, rowf, cb, cf, hh), rhs, preferred_element_type=F32)
                  for hh in (2 * k, 2 * k + 1)]
            h0 = quad * HQ + 2 * k
            skip = jnp.where(even, dsk_ref[h0], dsk_ref[h0 + 1])
            y = jnp.where(even, ys[0], ys[1]) + skip * x_pair.astype(F32)
            u_ref[r, pair_cols(k)] = (y * _silu(z_ref[r, pair_cols(k)].astype(F32))).astype(u_ref.dtype)
        return carry

    def pair_state(ref, d, k):
        return ref[d, 2 * k:2 * k + 2].reshape(2 * P_SSD, N_SSD).T

    def put_states(g, s_f, s_b):
        for k in range(HP):
            st_ref[g, 0, 2 * k:2 * k + 2] = s_f[k].T.reshape(2, P_SSD, N_SSD)
            st_ref[g, 1, 2 * k:2 * k + 2] = s_b[k].T.reshape(2, P_SSD, N_SSD)

    zero = tuple(jnp.zeros((N_SSD, 2 * P_SSD), F32) for _ in range(HP))
    if n_seq == 1:
        s0_f, s0_b = zero, zero
        if has_init:
            s0_f = tuple(pair_state(init_ref, 0, k) for k in range(HP))
            s0_b = tuple(pair_state(init_ref, 1, k) for k in range(HP))
        s_f, s_b = lax.fori_loop(0, nc, states, (s0_f, s0_b), unroll=min(nc, 2 * SSD_UNROLL))
        if want_state:
            put_states(0, s_f, s_b)
    else:
        assert not has_init

        def sequence(g, carry):
            s_f = s_b = zero
            for u in range(per):
                s_f, s_b = states(u, (s_f, s_b), base=g * per, last=per - 1)
            if want_state:
                put_states(g, s_f, s_b)
            return carry

        lax.fori_loop(0, n_seq, sequence, 0, unroll=min(n_seq, max(1, 2 * SSD_UNROLL // per)))
    lax.fori_loop(0, nc, outs, 0, unroll=min(nc, 2 * SSD_UNROLL))


def _ssd(proj3, dt3, d_skip, dt_bias_q, a_log_q, conv_w, conv_b, init, layer, state_buf, n_seq=1):
    b, seq, _ = proj3.shape
    want_state = state_buf is not None
    aliases = {}
    wq = HQ * P_SSD
    col = lambda width, off, fn: pl.BlockSpec((None, seq, width),
                                              lambda i, q, o=off // width: (i, 0, o + fn(q)))
    ident = lambda q: q
    group = lambda q: q // (NQ // G_SSD)
    xoff = 0
    boff = D_SSD
    coff = D_SSD + G_SSD * N_SSD
    cw = lambda width, off, fn: pl.BlockSpec((D_CONV, width), lambda i, q, o=off // width: (0, o + fn(q)))
    cbias = lambda width, off, fn: pl.BlockSpec((1, width), lambda i, q, o=off // width: (0, o + fn(q)))
    in_specs = [
        pl.BlockSpec(memory_space=pltpu.SMEM),
        col(wq, OFF_XS, ident), col(wq, OFF_Z, ident), col(N_SSD, OFF_B, group), col(N_SSD, OFF_C, group),
        pl.BlockSpec((None, seq, DT_LANES), lambda i, q: (i, 0, q)),
        pl.BlockSpec((None, 1, DT_LANES), lambda i, q: (q, 0, 0)),
        pl.BlockSpec((None, 1, DT_LANES), lambda i, q: (q, 0, 0)),
        cw(wq, xoff, ident), cbias(wq, xoff, ident),
        cw(N_SSD, boff, group), cbias(N_SSD, boff, group),
        cw(N_SSD, coff, group), cbias(N_SSD, coff, group),
    ]
    args = [d_skip, proj3, proj3, proj3, proj3, dt3, dt_bias_q, a_log_q,
            conv_w, conv_b, conv_w, conv_b, conv_w, conv_b]
    st_spec = pl.BlockSpec((None, None, 2, HQ, P_SSD, N_SSD), lambda i, q: (i, layer, 0, q, 0, 0))
    if init is not None:
        in_specs.append(st_spec)
        args.append(init)
    out_specs = [pl.BlockSpec((None, seq, wq), lambda i, q: (i, 0, q))]
    out_shape = [jax.ShapeDtypeStruct((b, seq, D_SSD), BF16)]
    if want_state:
        aliases[len(args)] = 1
        in_specs.append(pl.BlockSpec(memory_space=pl.ANY))
        args.append(state_buf)
        out_specs.append(pl.BlockSpec((n_seq, None, 2, HQ, P_SSD, N_SSD),
                                      lambda i, q: (i, layer, 0, q, 0, 0)))
        out_shape.append(jax.ShapeDtypeStruct(state_buf.shape, state_buf.dtype))
    nc = seq // CHUNK
    return pl.pallas_call(
        functools.partial(_ssd_kernel, seq=seq, n_seq=n_seq, has_init=init is not None,
                          want_state=want_state),
        grid=(b, NQ),
        in_specs=in_specs,
        out_specs=out_specs,
        out_shape=out_shape,
        scratch_shapes=[
            pltpu.VMEM((seq, wq), BF16), pltpu.VMEM((nc, N_SSD, CHUNK), BF16), pltpu.VMEM((seq, N_SSD), BF16),
            pltpu.VMEM((seq, DT_LANES), F32), pltpu.VMEM((nc, 4 * HQ, CHUNK), F32),
            pltpu.VMEM((nc * HP, N_SSD, 2 * P_SSD), BF16), pltpu.VMEM((nc * HP, N_SSD, 2 * P_SSD), BF16),
        ],
        input_output_aliases=aliases,
        compiler_params=_params(("arbitrary", "arbitrary")),
        name="ssd",
    )(*args)


def _out_kernel(yr_ref, us_ref, nw_ref, x_ref, g_ref, w_ref, o_ref):
    a = jnp.concatenate([yr_ref[...], _rms(us_ref[...].astype(F32), nw_ref[...]).astype(BF16)], axis=1)
    o_ref[...] = x_ref[...] + g_ref[...] * jnp.dot(a, w_ref[...], preferred_element_type=F32)


def _out_proj(yr, us, ssd_nw, x2, rows_per_cond, gate, w_out_all, layer):
    m = x2.shape[0]
    tm = 512
    return pl.pallas_call(
        _out_kernel,
        grid=(m // tm,),
        in_specs=[
            pl.BlockSpec((tm, D_RET), lambda i: (i, 0)),
            pl.BlockSpec((tm, D_SSD), lambda i: (i, 0)),
            pl.BlockSpec((1, D_SSD), lambda i: (0, 0)),
            pl.BlockSpec((tm, D_MODEL), lambda i: (i, 0)),
            pl.BlockSpec((None, 1, D_MODEL), lambda i: ((i * tm) // rows_per_cond, 0, 0)),
            pl.BlockSpec((None, D_RET + D_SSD, D_MODEL), lambda i: (layer, 0, 0)),
        ],
        out_specs=pl.BlockSpec((tm, D_MODEL), lambda i: (i, 0)),
        out_shape=jax.ShapeDtypeStruct((m, D_MODEL), F32),
        compiler_params=_params(("arbitrary",)),
        name="out_proj",
    )(yr, us, ssd_nw, x2, gate, w_out_all)


def _ffn_kernel(*refs, final, nf):
    it = iter(refs)
    x_ref, res_ref, nw_ref, sc_ref, sh_ref, g_ref, w1_ref, w2_ref = (next(it) for _ in range(8))
    fw_ref = next(it) if final else None
    o_ref, h_cur, h_next, acc_scr = next(it), next(it), next(it), next(it)
    i, f = pl.program_id(0), pl.program_id(1)
    part = x_ref.shape[0] // nf
    rows = pl.ds(pl.multiple_of(f * part, part), part)

    def stage():
        h_next[rows, :] = _modulated_norm(x_ref[rows, :], nw_ref[...], sc_ref[...], sh_ref[...])

    def step(first):
        stage()
        a = jnp.maximum(jnp.dot(h_cur[...], w1_ref[...], preferred_element_type=F32), 0.0)
        d = jnp.dot((a * a).astype(BF16), w2_ref[...], preferred_element_type=F32)
        if first:
            acc_scr[...] = d
        else:
            acc_scr[...] += d

    pl.when(i == 0)(stage)
    pl.when((i > 0) & (f == 0))(functools.partial(step, True))
    pl.when((i > 0) & (f > 0))(functools.partial(step, False))

    @pl.when((i > 0) & (f == nf - 1))
    def _():
        y = res_ref[...] + g_ref[...] * acc_scr[...]
        if final:
            y = _rms(y, fw_ref[...])
        o_ref[...] = y

    @pl.when(f == nf - 1)
    def _():
        h_cur[...] = h_next[...]


def _ffn(x2, rows_per_cond, nw, sc, sh, gate, w1, w2, layer, final_w):
    m = x2.shape[0]
    tm, tf = 512, 1024
    nt, nf = m // tm, D_FF // tf
    staged = lambda i: jnp.minimum(i, nt - 1)
    done = lambda i: jnp.maximum(i - 1, 0)
    wtile = lambda i, f: jnp.where(i > 0, f, 0)
    vec = pl.BlockSpec((1, D_MODEL), lambda i, f: (0, 0))
    cvec = lambda tile: pl.BlockSpec((None, 1, D_MODEL),
                                     lambda i, f: ((tile(i) * tm) // rows_per_cond, 0, 0))
    in_specs = [pl.BlockSpec((tm, D_MODEL), lambda i, f: (staged(i), 0)),
                pl.BlockSpec((tm, D_MODEL), lambda i, f: (done(i), 0)),
                vec, cvec(staged), cvec(staged), cvec(done),
                pl.BlockSpec((None, D_MODEL, tf), lambda i, f: (layer, 0, wtile(i, f))),
                pl.BlockSpec((None, tf, D_MODEL), lambda i, f: (layer, wtile(i, f), 0))]
    args = [x2, x2, nw, sc, sh, gate, w1, w2]
    if final_w is not None:
        in_specs.append(vec)
        args.append(final_w)
    return pl.pallas_call(
        functools.partial(_ffn_kernel, final=final_w is not None, nf=nf),
        grid=(nt + 1, nf),
        in_specs=in_specs,
        out_specs=pl.BlockSpec((tm, D_MODEL), lambda i, f: (done(i), 0)),
        out_shape=jax.ShapeDtypeStruct((m, D_MODEL), F32),
        scratch_shapes=[pltpu.VMEM((tm, D_MODEL), BF16), pltpu.VMEM((tm, D_MODEL), BF16),
                        pltpu.VMEM((tm, D_MODEL), F32)],
        compiler_params=_params(("arbitrary", "arbitrary")),
        name="ffn",
    )(*args)


def _rope_tables(seq):
    pos = jnp.arange(seq)
    row = (pos // GRID_W).astype(F32)
    col = (pos % GRID_W).astype(F32)
    half = DK_RET // 2
    inv = 1.0 / (ROPE_BASE ** (jnp.arange(0, half, 2, dtype=F32) / half))
    ang = jnp.concatenate([row[:, None] * inv, col[:, None] * inv], -1)
    cs, sn = jnp.cos(ang), jnp.sin(ang)
    return jnp.concatenate([cs, cs], -1), jnp.concatenate([-sn, sn], -1)


def _quad_lanes(v):
    pad = jnp.zeros((DT_LANES - 4 * HQ,), v.dtype)
    quads = []
    for q in range(NQ):
        fb = [v[0, q * HQ:(q + 1) * HQ], v[1, q * HQ:(q + 1) * HQ]]
        quads.append(jnp.concatenate(fb + fb + [pad]))
    return jnp.stack(quads)[:, None, :]


def _dt_weight(w_dt):
    parts = []
    for q in range(NQ):
        fb = [w_dt[..., q * HQ:(q + 1) * HQ], w_dt[..., H_SSD + q * HQ:H_SSD + (q + 1) * HQ]]
        parts += fb + fb
    parts.append(jnp.zeros(w_dt.shape[:-1] + (DT_LANES - NQ * QUAD_LANES,), w_dt.dtype))
    return jnp.concatenate(parts, axis=-1)


def kernel(x_prompt, x_sample, state_ret, state_ssd, c, c_ctx, w_ada, b_ada, norm1_w, w_in,
           ret_log_decay, conv_w, conv_b, dt_bias, a_log, d_skip, ssd_norm_w, w_out, norm2_w,
           w_ff1, w_ff2, final_norm_w):
    bp, sp, _ = x_prompt.shape
    bs, ss, _ = x_sample.shape

    cond = jnp.zeros((COND_ROWS, D_MODEL), F32).at[:bs].set(c).at[bs].set(c_ctx)
    mod = _ada(cond, w_ada, b_ada).reshape(DEPTH, COND_ROWS, N_MOD, D_MODEL)

    rope = _rope_tables(ss)
    final_w = final_norm_w.reshape(1, D_MODEL)
    w_in_b = w_in.astype(BF16)
    w_dt_b = _dt_weight(w_in[:, :, D_MAIN:]).astype(BF16)
    w_out_b = w_out.astype(BF16)
    w_ff1_b = w_ff1.astype(BF16)
    w_ff2_b = w_ff2.astype(BF16)

    def run_group(x, mod_rows, rope_tabs, states, new_states):
        b, seq, _ = x.shape
        x2 = x.reshape(b * seq, D_MODEL)
        rows_per_cond = seq if mod_rows.stop - mod_rows.start > 1 else b * seq
        new_ret, new_ssd = new_states
        for l in range(DEPTH):
            mv = [mod[l, mod_rows, k][:, None, :] for k in range(N_MOD)]
            sh1, sc1, g1, sh2, sc2, g2 = mv
            proj, dt = _in_proj(x2, rows_per_cond, norm1_w[l].reshape(1, D_MODEL), sc1, sh1,
                                w_in_b, l, w_dt_b[l])
            init_r, init_s = (None, None) if states is None else states
            blocks, n_seq = (b, 1) if states is not None else (1, b)
            proj3 = proj.reshape(blocks, n_seq * seq, D_MAIN)
            dt3 = dt.reshape(blocks, n_seq * seq, NQ * DT_LANES)
            ret = _retention(proj3, ret_log_decay[l], rope_tabs, init_r, l, new_ret, n_seq=n_seq)
            ssd = _ssd(proj3, dt3, d_skip[l], _quad_lanes(dt_bias[l]), _quad_lanes(a_log[l]),
                       conv_w[l], conv_b[l].reshape(1, D_XBC), init_s, l, new_ssd, n_seq=n_seq)
            if new_ret is not None:
                new_ret, new_ssd = ret[1], ssd[1]
            x2 = _out_proj(ret[0].reshape(b * seq, D_RET), ssd[0].reshape(b * seq, D_SSD),
                           ssd_norm_w[l].reshape(1, D_SSD), x2, rows_per_cond, g1, w_out_b, l)
            x2 = _ffn(x2, rows_per_cond, norm2_w[l].reshape(1, D_MODEL), sc2, sh2, g2,
                      w_ff1_b, w_ff2_b, l, final_w if l == DEPTH - 1 else None)
        return x2.reshape(b, seq, D_MODEL), new_ret, new_ssd

    empty_states = (jnp.zeros((bp,) + state_ret.shape[1:], F32), jnp.zeros((bp,) + state_ssd.shape[1:], F32))
    y_prompt, new_state_ret, new_state_ssd = run_group(x_prompt, slice(bs, bs + 1), None, None, empty_states)
    y_sample, _, _ = run_group(x_sample, slice(0, bs), rope, (state_ret, state_ssd), (None, None))
    return (y_prompt, y_sample, new_state_ret, new_state_ssd)
```

```python
import functools

import jax
import jax.numpy as jnp
from jax import lax
from jax.experimental import pallas as pl
from jax.experimental.pallas import tpu as pltpu

F32 = jnp.float32
BF16 = jnp.bfloat16

D_MODEL = 2048
DEPTH = 4
CHUNK = 128
H_RET = 8
DK_RET = 128
DV_RET = 128
D_RET = H_RET * DV_RET
H_SSD = 16
P_SSD = 64
D_SSD = H_SSD * P_SSD
G_SSD = 2
N_SSD = 128
D_CONV = 3
D_XBC = D_SSD + 2 * G_SSD * N_SSD
D_FF = 4 * D_MODEL
N_MOD = 6
EPS = 1e-6
ROPE_BASE = 10000.0
GRID_W = 64

D_MAIN = 4 * D_RET + D_SSD + D_XBC
OFF_Q, OFF_K, OFF_V, OFF_G = 0, D_RET, 2 * D_RET, 3 * D_RET
OFF_Z = 4 * D_RET
OFF_XS = OFF_Z + D_SSD
OFF_B = OFF_XS + D_SSD
OFF_C = OFF_B + G_SSD * N_SSD

HQ = 4
NQ = H_SSD // HQ
HP = HQ // 2
DT_LANES = 128
QUAD_LANES = 4 * HQ
COND_ROWS = 16
SCAN_UNROLL = 32
SSD_UNROLL = 4
V7X_VMEM_LIMIT = 56 * 1024 * 1024


def _silu(x):
    return x * jax.nn.sigmoid(x)


def _softplus(x):
    return jnp.maximum(x, 0.0) + jnp.log1p(jnp.exp(-jnp.abs(x)))


def _rms(x, w):
    return x * lax.rsqrt(jnp.mean(x * x, axis=-1, keepdims=True) + EPS) * w


def _params(sem):
    return pltpu.CompilerParams(dimension_semantics=sem, vmem_limit_bytes=V7X_VMEM_LIMIT)


def _ada_kernel(cond_ref, w_ref, b_ref, o_ref):
    s = _silu(cond_ref[...]).astype(BF16)
    o_ref[...] = jnp.dot(s, w_ref[...].astype(BF16), preferred_element_type=F32) + b_ref[...]


def _ada(cond, w_ada, b_ada):
    tn = 2048
    n = N_MOD * D_MODEL
    return pl.pallas_call(
        _ada_kernel,
        grid=(DEPTH, n // tn),
        in_specs=[
            pl.BlockSpec((COND_ROWS, D_MODEL), lambda l, j: (0, 0)),
            pl.BlockSpec((None, D_MODEL, tn), lambda l, j: (l, 0, j)),
            pl.BlockSpec((None, 1, tn), lambda l, j: (l, 0, j)),
        ],
        out_specs=pl.BlockSpec((None, COND_ROWS, tn), lambda l, j: (l, 0, j)),
        out_shape=jax.ShapeDtypeStruct((DEPTH, COND_ROWS, n), F32),
        compiler_params=_params(("arbitrary", "arbitrary")),
        name="ada_mod",
    )(cond, w_ada, b_ada.reshape(DEPTH, 1, n))


def _modulated_norm(x, nw, sc, sh):
    return (_rms(x, nw) * (1.0 + sc) + sh).astype(BF16)


def _in_kernel(x_ref, nw_ref, sc_ref, sh_ref, w_ref, wdt_ref, o_ref, dt_ref, h_cur, h_next, *, nj):
    i, j = pl.program_id(0), pl.program_id(1)
    part = x_ref.shape[0] // nj
    rows = pl.ds(pl.multiple_of(j * part, part), part)

    def stage():
        h_next[rows, :] = _modulated_norm(x_ref[rows, :], nw_ref[...], sc_ref[...], sh_ref[...])

    @pl.when(i == 0)
    def _():
        stage()

    @pl.when(i > 0)
    def _():
        stage()
        dt = jnp.dot(h_cur[rows, :], wdt_ref[...], preferred_element_type=F32)
        for q in range(NQ):
            shift = (DT_LANES - QUAD_LANES * q) % DT_LANES
            dt_ref[rows, q * DT_LANES:(q + 1) * DT_LANES] = pltpu.roll(dt, shift, 1) if shift else dt
        o_ref[...] = jnp.dot(h_cur[...], w_ref[...], preferred_element_type=F32).astype(o_ref.dtype)

    @pl.when(j == nj - 1)
    def _():
        h_cur[...] = h_next[...]


def _in_proj(x2, rows_per_cond, nw, sc, sh, w_in_all, layer, w_dt):
    m = x2.shape[0]
    tm, tn = 512, D_MAIN // 2
    nt = m // tm
    staged = lambda i: jnp.minimum(i, nt - 1)
    done = lambda i: jnp.maximum(i - 1, 0)
    cond_map = lambda i, j: ((staged(i) * tm) // rows_per_cond, 0, 0)
    return pl.pallas_call(
        functools.partial(_in_kernel, nj=D_MAIN // tn),
        grid=(nt + 1, D_MAIN // tn),
        in_specs=[
            pl.BlockSpec((tm, D_MODEL), lambda i, j: (staged(i), 0)),
            pl.BlockSpec((1, D_MODEL), lambda i, j: (0, 0)),
            pl.BlockSpec((None, 1, D_MODEL), cond_map),
            pl.BlockSpec((None, 1, D_MODEL), cond_map),
            pl.BlockSpec((None, D_MODEL, tn), lambda i, j: (layer, 0, jnp.where(i > 0, j, 0))),
            pl.BlockSpec((D_MODEL, DT_LANES), lambda i, j: (0, 0)),
        ],
        out_specs=[
            pl.BlockSpec((tm, tn), lambda i, j: (done(i), jnp.where(i > 0, j, 0))),
            pl.BlockSpec((tm, NQ * DT_LANES), lambda i, j: (done(i), 0)),
        ],
        out_shape=[
            jax.ShapeDtypeStruct((m, D_MAIN), BF16),
            jax.ShapeDtypeStruct((m, NQ * DT_LANES), F32),
        ],
        scratch_shapes=[pltpu.VMEM((tm, D_MODEL), BF16), pltpu.VMEM((tm, D_MODEL), BF16)],
        compiler_params=_params(("arbitrary", "arbitrary")),
        name="in_proj",
    )(x2, nw, sc, sh, w_in_all, w_dt)


def _ret_kernel(*refs, seq, n_seq, rope, has_init, want_state):
    it = iter(refs)
    ld_ref, q_ref, k_ref, v_ref, g_ref = (next(it) for _ in range(5))
    cos_ref = sin_ref = init_ref = st_ref = None
    if rope:
        cos_ref, sin_ref = next(it), next(it)
    if has_init:
        init_ref = next(it)
    if want_state:
        next(it)
    y_ref = next(it)
    if want_state:
        st_ref = next(it)
    q_scr, kt_scr, sf_scr, sb_scr = next(it), next(it), next(it), next(it)

    t = CHUNK
    nc = seq // t
    head = pl.program_id(1)
    la_f = ld_ref[0, head]
    la_b = ld_ref[1, head]

    ii = lax.broadcasted_iota(jnp.int32, (t, t), 0)
    jj = lax.broadcasted_iota(jnp.int32, (t, t), 1)
    dist = (ii - jj).astype(F32)
    lower = ii >= jj
    upper = ii <= jj
    dmat = (jnp.where(lower, jnp.exp(la_f * jnp.where(lower, dist, 0.0)), 0.0)
            + jnp.where(upper, jnp.exp(la_b * jnp.where(upper, -dist, 0.0)), 0.0))
    pos = lax.broadcasted_iota(jnp.int32, (t, 1), 0).astype(F32)
    lane = lax.broadcasted_iota(jnp.int32, (1, t), 1).astype(F32)
    e_f = jnp.exp(la_f * (pos + 1.0))
    e_b = jnp.exp(la_b * (t - pos))
    w_f = jnp.exp(la_f * (t - 1.0 - lane))
    w_b = jnp.exp(la_b * lane)
    dec_f = jnp.exp(jnp.full((1, 1), t, F32) * la_f)
    dec_b = jnp.exp(jnp.full((1, 1), t, F32) * la_b)
    scale = DK_RET ** -0.5
    unroll = min(nc, SCAN_UNROLL)

    def rows(c):
        return pl.ds(pl.multiple_of(c * t, t), t)

    def prep(c, carry):
        r = rows(c)
        q = q_ref[r, :].astype(F32)
        k = k_ref[r, :].astype(F32)
        if rope:
            cs, sn = cos_ref[r, :], sin_ref[r, :]
            q = q * cs + pltpu.roll(q, DK_RET // 2, 1) * sn
            k = k * cs + pltpu.roll(k, DK_RET // 2, 1) * sn
        q_scr[r, :] = (q * scale).astype(BF16)
        kt_scr[c] = k.T.astype(BF16)
        return carry

    lax.fori_loop(0, nc, prep, 0, unroll=unroll)

    def chunk_state(c, w_row):
        kw = (kt_scr[c].astype(F32) * w_row).astype(BF16)
        return jnp.dot(kw, v_ref[rows(c), :], preferred_element_type=F32)

    def states(i, carry, base=0, last=nc - 1):
        s_f, s_b = carry
        c_f, c_b = base + i, base + last - i
        sf_scr[c_f] = s_f.astype(BF16)
        sb_scr[c_b] = s_b.astype(BF16)
        return dec_f * s_f + chunk_state(c_f, w_f), dec_b * s_b + chunk_state(c_b, w_b)

    def outs(c, carry):
        r = rows(c)
        q = q_scr[r, :]
        att = jnp.dot(q, kt_scr[c], preferred_element_type=F32)
        qf = q.astype(F32)
        a = jnp.concatenate([(att * dmat).astype(BF16), (qf * e_f).astype(BF16),
                             (qf * e_b).astype(BF16)], axis=1)
        rhs = jnp.concatenate([v_ref[r, :], sf_scr[c], sb_scr[c]], axis=0)
        y = jnp.dot(a, rhs, preferred_element_type=F32)
        yn = y * lax.rsqrt(jnp.mean(y * y, axis=-1, keepdims=True) + EPS)
        y_ref[r, :] = (_silu(g_ref[r, :].astype(F32)) * yn).astype(y_ref.dtype)
        return carry

    zero = jnp.zeros((DK_RET, DV_RET), F32)
    if n_seq == 1:
        s0_f, s0_b = (init_ref[0].T, init_ref[1].T) if has_init else (zero, zero)
        s_f, s_b = lax.fori_loop(0, nc, states, (s0_f, s0_b), unroll=unroll)
        if want_state:
            st_ref[0, 0] = s_f.T
            st_ref[0, 1] = s_b.T
    else:
        assert not has_init and not rope
        per = nc // n_seq

        def sequence(g, carry):
            s_f = s_b = zero
            for u in range(per):
                s_f, s_b = states(u, (s_f, s_b), base=g * per, last=per - 1)
            if want_state:
                st_ref[g, 0] = s_f.T
                st_ref[g, 1] = s_b.T
            return carry

        lax.fori_loop(0, n_seq, sequence, 0, unroll=min(n_seq, max(1, SCAN_UNROLL // per)))
    lax.fori_loop(0, nc, outs, 0, unroll=unroll)


def _retention(proj3, ret_ld, rope_tabs, init, layer, state_buf, n_seq=1):
    b, seq, _ = proj3.shape
    want_state = state_buf is not None
    aliases = {}
    hblk = lambda off: pl.BlockSpec((None, seq, DK_RET), lambda i, h, o=off // DK_RET: (i, 0, o + h))
    in_specs = [pl.BlockSpec(memory_space=pltpu.SMEM), hblk(OFF_Q), hblk(OFF_K), hblk(OFF_V), hblk(OFF_G)]
    args = [ret_ld, proj3, proj3, proj3, proj3]
    if rope_tabs is not None:
        in_specs += [pl.BlockSpec((seq, DK_RET), lambda i, h: (0, 0))] * 2
        args += list(rope_tabs)
    if init is not None:
        in_specs.append(pl.BlockSpec((None, None, 2, None, DV_RET, DK_RET),
                                     lambda i, h: (i, layer, 0, h, 0, 0)))
        args.append(init)
    out_specs = [pl.BlockSpec((None, seq, DV_RET), lambda i, h: (i, 0, h))]
    out_shape = [jax.ShapeDtypeStruct((b, seq, D_RET), BF16)]
    if want_state:
        aliases[len(args)] = 1
        in_specs.append(pl.BlockSpec(memory_space=pl.ANY))
        args.append(state_buf)
        out_specs.append(pl.BlockSpec((n_seq, None, 2, None, DV_RET, DK_RET),
                                      lambda i, h: (i, layer, 0, h, 0, 0)))
        out_shape.append(jax.ShapeDtypeStruct(state_buf.shape, state_buf.dtype))
    return pl.pallas_call(
        functools.partial(_ret_kernel, seq=seq, n_seq=n_seq, rope=rope_tabs is not None,
                          has_init=init is not None, want_state=want_state),
        grid=(b, H_RET),
        in_specs=in_specs,
        out_specs=out_specs,
        out_shape=out_shape,
        scratch_shapes=[pltpu.VMEM((seq, DK_RET), BF16),
                        pltpu.VMEM((seq // CHUNK, DK_RET, CHUNK), BF16),
                        pltpu.VMEM((seq // CHUNK, DK_RET, DV_RET), BF16),
                        pltpu.VMEM((seq // CHUNK, DK_RET, DV_RET), BF16)],
        input_output_aliases=aliases,
        compiler_params=_params(("arbitrary", "arbitrary")),
        name="retention",
    )(*args)


def _conv_silu_chunk(src_ref, w_ref, b_ref, c, nc, per):
    t = CHUNK
    r0 = pl.multiple_of(c * t, t)
    cur = src_ref[pl.ds(r0, t), :].astype(F32)
    p0 = pl.multiple_of(jnp.maximum(r0 - 16, 0), 16)
    n0 = pl.multiple_of(jnp.minimum(r0 + t, nc * t - 16), 16)
    pos = lax.rem(c, per)
    prev = jnp.where(pos > 0, src_ref[pl.ds(p0, 16), :].astype(F32)[15:16, :], 0.0)
    nxt = jnp.where(pos < per - 1, src_ref[pl.ds(n0, 16), :].astype(F32)[0:1, :], 0.0)
    row = lax.broadcasted_iota(jnp.int32, (t, 1), 0)
    xm1 = jnp.where(row == 0, prev, pltpu.roll(cur, 1, 0))
    xp1 = jnp.where(row == t - 1, nxt, pltpu.roll(cur, t - 1, 0))
    return _silu(xm1 * w_ref[0:1, :] + cur * w_ref[1:2, :] + xp1 * w_ref[2:3, :] + b_ref[...])


def _exact_cumsum(tri_b, x):
    hi = x.astype(BF16)
    r1 = x - hi.astype(F32)
    mid = r1.astype(BF16)
    lo = (r1 - mid.astype(F32)).astype(BF16)
    return (jnp.dot(tri_b, hi, preferred_element_type=F32)
            + jnp.dot(tri_b, mid, preferred_element_type=F32)
            + jnp.dot(tri_b, lo, preferred_element_type=F32))


def _ssd_kernel(*refs, seq, n_seq, has_init, want_state):
    it = iter(refs)
    (dsk_ref, xs_ref, z_ref, b_ref, c_ref, dt_ref, bias_ref, alog_ref,
     wx_ref, bx_ref, wb_ref, bb_ref, wc_ref, bc_ref) = (next(it) for _ in range(14))
    init_ref = st_ref = None
    if has_init:
        init_ref = next(it)
    if want_state:
        next(it)
    u_ref = next(it)
    if want_state:
        st_ref = next(it)
    xc_scr, bt_scr, cc_scr, col_scr, row_scr, sf_scr, sb_scr = (next(it) for _ in range(7))

    t = CHUNK
    nc = seq // t
    per = nc // n_seq
    quad = pl.program_id(1)
    unroll = min(nc, SSD_UNROLL)

    ii = lax.broadcasted_iota(jnp.int32, (t, t), 0)
    jj = lax.broadcasted_iota(jnp.int32, (t, t), 1)
    lower = ii >= jj
    strict_lower = ii > jj
    diag = ii == jj
    tril_b = jnp.where(lower, 1.0, 0.0).astype(BF16)
    lane = lax.broadcasted_iota(jnp.int32, (1, DT_LANES), 1)
    a_neg = -jnp.exp(alog_ref[...])
    dt_bias = bias_ref[...]

    def rows(c):
        return pl.ds(pl.multiple_of(c * t, t), t)

    def prep_group(c, carry):
        bt_scr[c] = _conv_silu_chunk(b_ref, wb_ref, bb_ref, c, nc, per).T.astype(BF16)
        cc_scr[rows(c), :] = _conv_silu_chunk(c_ref, wc_ref, bc_ref, c, nc, per).astype(BF16)
        return carry

    @pl.when(quad % (NQ // G_SSD) == 0)
    def _():
        lax.fori_loop(0, nc, prep_group, 0, unroll=unroll)

    def prep(c, carry):
        r = rows(c)
        xc_scr[r, :] = _conv_silu_chunk(xs_ref, wx_ref, bx_ref, c, nc, per).astype(BF16)
        dt = _softplus(dt_ref[r, :] + dt_bias)
        da = dt * a_neg
        cum = _exact_cumsum(tril_b, da)
        total = cum[t - 1:t, :]
        decay = jnp.where(lane < HQ, cum, total - cum + da)
        packed = jnp.where(lane < 2 * HQ, decay, dt)
        col_scr[r, :] = packed
        row_scr[c] = packed.T[0:4 * HQ, :]
        return carry

    lax.fori_loop(0, nc, prep, 0, unroll=min(nc, 2 * SSD_UNROLL))

    even = lax.broadcasted_iota(jnp.int32, (1, 2 * P_SSD), 1) < P_SSD

    def pair_cols(k):
        return slice(k * 2 * P_SSD, (k + 1) * 2 * P_SSD)

    def chunk_states(c, first, tot_lane):
        rowf = row_scr[c]
        cum_r = rowf[first:first + HQ, :]
        dt_r = rowf[2 * HQ + first:3 * HQ + first, :]
        tot = cum_r[:, tot_lane:tot_lane + 1]
        w = jnp.exp(tot - cum_r) * dt_r
        dec = jnp.exp(tot)
        btf = bt_scr[c].astype(F32)
        r = rows(c)
        decs, css = [], []
        for k in range(HP):
            x_pair = xc_scr[r, pair_cols(k)]
            cs = [jnp.dot((btf * w[hh:hh + 1, :]).astype(BF16), x_pair, preferred_element_type=F32)
                  for hh in (2 * k, 2 * k + 1)]
            css.append(jnp.where(even, cs[0], cs[1]))
            decs.append(jnp.where(even, dec[2 * k:2 * k + 1, :], dec[2 * k + 1:2 * k + 2, :]))
        return decs, css

    def states(i, carry, base=0, last=nc - 1):
        s_f, s_b = carry
        c_f, c_b = base + i, base + last - i
        dec_f, cs_f = chunk_states(c_f, 0, t - 1)
        dec_b, cs_b = chunk_states(c_b, HQ, 0)
        new_f, new_b = [], []
        for k in range(HP):
            sf_scr[c_f * HP + k] = s_f[k].astype(BF16)
            sb_scr[c_b * HP + k] = s_b[k].astype(BF16)
            new_f.append(dec_f[k] * s_f[k] + cs_f[k])
            new_b.append(dec_b[k] * s_b[k] + cs_b[k])
        return tuple(new_f), tuple(new_b)

    def head_lhs(colf, rowf, cb, cf, hh):
        fl, bl = hh, HQ + hh
        rc_f = jnp.broadcast_to(colf[:, fl:fl + 1], (t, t))
        rc_b = jnp.broadcast_to(colf[:, bl:bl + 1], (t, t))
        seg = jnp.where(lower, rc_f - rowf[fl:fl + 1, :], rc_b - rowf[bl:bl + 1, :])
        dt_f = rowf[2 * HQ + fl:2 * HQ + fl + 1, :]
        dt_b = rowf[2 * HQ + bl:2 * HQ + bl + 1, :]
        dt_sel = jnp.where(strict_lower, dt_f, jnp.where(diag, dt_f + dt_b, dt_b))
        return jnp.concatenate([(cb * (jnp.exp(seg) * dt_sel)).astype(BF16),
                                (cf * jnp.exp(rc_f)).astype(BF16),
                                (cf * jnp.exp(rc_b)).astype(BF16)], axis=1)

    def outs(c, carry):
        r = rows(c)
        colf = col_scr[r, :]
        rowf = row_scr[c]
        cmat = cc_scr[r, :]
        cf = cmat.astype(F32)
        cb = jnp.dot(cmat, bt_scr[c], preferred_element_type=F32)
        for k in range(HP):
            x_pair = xc_scr[r, pair_cols(k)]
            rhs = jnp.concatenate([x_pair, sf_scr[c * HP + k], sb_scr[c * HP + k]], axis=0)
            ys = [jnp.dot(head_lhs(colf, rowf, cb, cf, hh), rhs, preferred_element_type=F32)
                  for hh in (2 * k, 2 * k + 1)]
            h0 = quad * HQ + 2 * k
            skip = jnp.where(even, dsk_ref[h0], dsk_ref[h0 + 1])
            y = jnp.where(even, ys[0], ys[1]) + skip * x_pair.astype(F32)
            u_ref[r, pair_cols(k)] = (y * _silu(z_ref[r, pair_cols(k)].astype(F32))).astype(u_ref.dtype)
        return carry

    def pair_state(ref, d, k):
        return ref[d, 2 * k:2 * k + 2].reshape(2 * P_SSD, N_SSD).T

    def put_states(g, s_f, s_b):
        for k in range(HP):
            st_ref[g, 0, 2 * k:2 * k + 2] = s_f[k].T.reshape(2, P_SSD, N_SSD)
            st_ref[g, 1, 2 * k:2 * k + 2] = s_b[k].T.reshape(2, P_SSD, N_SSD)

    zero = tuple(jnp.zeros((N_SSD, 2 * P_SSD), F32) for _ in range(HP))
    if n_seq == 1:
        s0_f, s0_b = zero, zero
        if has_init:
            s0_f = tuple(pair_state(init_ref, 0, k) for k in range(HP))
            s0_b = tuple(pair_state(init_ref, 1, k) for k in range(HP))
        s_f, s_b = lax.fori_loop(0, nc, states, (s0_f, s0_b), unroll=min(nc, 2 * SSD_UNROLL))
        if want_state:
            put_states(0, s_f, s_b)
    else:
        assert not has_init

        def sequence(g, carry):
            s_f = s_b = zero
            for u in range(per):
                s_f, s_b = states(u, (s_f, s_b), base=g * per, last=per - 1)
            if want_state:
                put_states(g, s_f, s_b)
            return carry

        lax.fori_loop(0, n_seq, sequence, 0, unroll=min(n_seq, max(1, 2 * SSD_UNROLL // per)))
    lax.fori_loop(0, nc, outs, 0, unroll=min(nc, 2 * SSD_UNROLL))


def _ssd(proj3, dt3, d_skip, dt_bias_q, a_log_q, conv_w, conv_b, init, layer, state_buf, n_seq=1):
    b, seq, _ = proj3.shape
    want_state = state_buf is not None
    aliases = {}
    wq = HQ * P_SSD
    col = lambda width, off, fn: pl.BlockSpec((None, seq, width),
                                              lambda i, q, o=off // width: (i, 0, o + fn(q)))
    ident = lambda q: q
    group = lambda q: q // (NQ // G_SSD)
    xoff = 0
    boff = D_SSD
    coff = D_SSD + G_SSD * N_SSD
    cw = lambda width, off, fn: pl.BlockSpec((D_CONV, width), lambda i, q, o=off // width: (0, o + fn(q)))
    cbias = lambda width, off, fn: pl.BlockSpec((1, width), lambda i, q, o=off // width: (0, o + fn(q)))
    in_specs = [
        pl.BlockSpec(memory_space=pltpu.SMEM),
        col(wq, OFF_XS, ident), col(wq, OFF_Z, ident), col(N_SSD, OFF_B, group), col(N_SSD, OFF_C, group),
        pl.BlockSpec((None, seq, DT_LANES), lambda i, q: (i, 0, q)),
        pl.BlockSpec((None, 1, DT_LANES), lambda i, q: (q, 0, 0)),
        pl.BlockSpec((None, 1, DT_LANES), lambda i, q: (q, 0, 0)),
        cw(wq, xoff, ident), cbias(wq, xoff, ident),
        cw(N_SSD, boff, group), cbias(N_SSD, boff, group),
        cw(N_SSD, coff, group), cbias(N_SSD, coff, group),
    ]
    args = [d_skip, proj3, proj3, proj3, proj3, dt3, dt_bias_q, a_log_q,
            conv_w, conv_b, conv_w, conv_b, conv_w, conv_b]
    st_spec = pl.BlockSpec((None, None, 2, HQ, P_SSD, N_SSD), lambda i, q: (i, layer, 0, q, 0, 0))
    if init is not None:
        in_specs.append(st_spec)
        args.append(init)
    out_specs = [pl.BlockSpec((None, seq, wq), lambda i, q: (i, 0, q))]
    out_shape = [jax.ShapeDtypeStruct((b, seq, D_SSD), BF16)]
    if want_state:
        aliases[len(args)] = 1
        in_specs.append(pl.BlockSpec(memory_space=pl.ANY))
        args.append(state_buf)
        out_specs.append(pl.BlockSpec((n_seq, None, 2, HQ, P_SSD, N_SSD),
                                      lambda i, q: (i, layer, 0, q, 0, 0)))
        out_shape.append(jax.ShapeDtypeStruct(state_buf.shape, state_buf.dtype))
    nc = seq // CHUNK
    return pl.pallas_call(
        functools.partial(_ssd_kernel, seq=seq, n_seq=n_seq, has_init=init is not None,
                          want_state=want_state),
        grid=(b, NQ),
        in_specs=in_specs,
        out_specs=out_specs,
        out_shape=out_shape,
        scratch_shapes=[
            pltpu.VMEM((seq, wq), BF16), pltpu.VMEM((nc, N_SSD, CHUNK), BF16), pltpu.VMEM((seq, N_SSD), BF16),
            pltpu.VMEM((seq, DT_LANES), F32), pltpu.VMEM((nc, 4 * HQ, CHUNK), F32),
            pltpu.VMEM((nc * HP, N_SSD, 2 * P_SSD), BF16), pltpu.VMEM((nc * HP, N_SSD, 2 * P_SSD), BF16),
        ],
        input_output_aliases=aliases,
        compiler_params=_params(("arbitrary", "arbitrary")),
        name="ssd",
    )(*args)


def _out_kernel(yr_ref, us_ref, nw_ref, x_ref, g_ref, w_ref, o_ref):
    a = jnp.concatenate([yr_ref[...], _rms(us_ref[...].astype(F32), nw_ref[...]).astype(BF16)], axis=1)
    o_ref[...] = x_ref[...] + g_ref[...] * jnp.dot(a, w_ref[...], preferred_element_type=F32)


def _out_proj(yr, us, ssd_nw, x2, rows_per_cond, gate, w_out_all, layer):
    m = x2.shape[0]
    tm = 512
    return pl.pallas_call(
        _out_kernel,
        grid=(m // tm,),
        in_specs=[
            pl.BlockSpec((tm, D_RET), lambda i: (i, 0)),
            pl.BlockSpec((tm, D_SSD), lambda i: (i, 0)),
            pl.BlockSpec((1, D_SSD), lambda i: (0, 0)),
            pl.BlockSpec((tm, D_MODEL), lambda i: (i, 0)),
            pl.BlockSpec((None, 1, D_MODEL), lambda i: ((i * tm) // rows_per_cond, 0, 0)),
            pl.BlockSpec((None, D_RET + D_SSD, D_MODEL), lambda i: (layer, 0, 0)),
        ],
        out_specs=pl.BlockSpec((tm, D_MODEL), lambda i: (i, 0)),
        out_shape=jax.ShapeDtypeStruct((m, D_MODEL), F32),
        compiler_params=_params(("arbitrary",)),
        name="out_proj",
    )(yr, us, ssd_nw, x2, gate, w_out_all)


def _ffn_kernel(*refs, final, nf):
    it = iter(refs)
    x_ref, res_ref, nw_ref, sc_ref, sh_ref, g_ref, w1_ref, w2_ref = (next(it) for _ in range(8))
    fw_ref = next(it) if final else None
    o_ref, h_cur, h_next, acc_scr = next(it), next(it), next(it), next(it)
    i, f = pl.program_id(0), pl.program_id(1)
    part = x_ref.shape[0] // nf
    rows = pl.ds(pl.multiple_of(f * part, part), part)

    def stage():
        h_next[rows, :] = _modulated_norm(x_ref[rows, :], nw_ref[...], sc_ref[...], sh_ref[...])

    def step(first):
        stage()
        a = jnp.maximum(jnp.dot(h_cur[...], w1_ref[...], preferred_element_type=F32), 0.0)
        d = jnp.dot((a * a).astype(BF16), w2_ref[...], preferred_element_type=F32)
        if first:
            acc_scr[...] = d
        else:
            acc_scr[...] += d

    pl.when(i == 0)(stage)
    pl.when((i > 0) & (f == 0))(functools.partial(step, True))
    pl.when((i > 0) & (f > 0))(functools.partial(step, False))

    @pl.when((i > 0) & (f == nf - 1))
    def _():
        y = res_ref[...] + g_ref[...] * acc_scr[...]
        if final:
            y = _rms(y, fw_ref[...])
        o_ref[...] = y

    @pl.when(f == nf - 1)
    def _():
        h_cur[...] = h_next[...]


def _ffn(x2, rows_per_cond, nw, sc, sh, gate, w1, w2, layer, final_w):
    m = x2.shape[0]
    tm, tf = 512, 1024
    nt, nf = m // tm, D_FF // tf
    staged = lambda i: jnp.minimum(i, nt - 1)
    done = lambda i: jnp.maximum(i - 1, 0)
    wtile = lambda i, f: jnp.where(i > 0, f, 0)
    vec = pl.BlockSpec((1, D_MODEL), lambda i, f: (0, 0))
    cvec = lambda tile: pl.BlockSpec((None, 1, D_MODEL),
                                     lambda i, f: ((tile(i) * tm) // rows_per_cond, 0, 0))
    in_specs = [pl.BlockSpec((tm, D_MODEL), lambda i, f: (staged(i), 0)),
                pl.BlockSpec((tm, D_MODEL), lambda i, f: (done(i), 0)),
                vec, cvec(staged), cvec(staged), cvec(done),
                pl.BlockSpec((None, D_MODEL, tf), lambda i, f: (layer, 0, wtile(i, f))),
                pl.BlockSpec((None, tf, D_MODEL), lambda i, f: (layer, wtile(i, f), 0))]
    args = [x2, x2, nw, sc, sh, gate, w1, w2]
    if final_w is not None:
        in_specs.append(vec)
        args.append(final_w)
    return pl.pallas_call(
        functools.partial(_ffn_kernel, final=final_w is not None, nf=nf),
        grid=(nt + 1, nf),
        in_specs=in_specs,
        out_specs=pl.BlockSpec((tm, D_MODEL), lambda i, f: (done(i), 0)),
        out_shape=jax.ShapeDtypeStruct((m, D_MODEL), F32),
        scratch_shapes=[pltpu.VMEM((tm, D_MODEL), BF16), pltpu.VMEM((tm, D_MODEL), BF16),
                        pltpu.VMEM((tm, D_MODEL), F32)],
        compiler_params=_params(("arbitrary", "arbitrary")),
        name="ffn",
    )(*args)


def _rope_tables(seq):
    pos = jnp.arange(seq)
    row = (pos // GRID_W).astype(F32)
    col = (pos % GRID_W).astype(F32)
    half = DK_RET // 2
    inv = 1.0 / (ROPE_BASE ** (jnp.arange(0, half, 2, dtype=F32) / half))
    ang = jnp.concatenate([row[:, None] * inv, col[:, None] * inv], -1)
    cs, sn = jnp.cos(ang), jnp.sin(ang)
    return jnp.concatenate([cs, cs], -1), jnp.concatenate([-sn, sn], -1)


def _quad_lanes(v):
    pad = jnp.zeros((DT_LANES - 4 * HQ,), v.dtype)
    quads = []
    for q in range(NQ):
        fb = [v[0, q * HQ:(q + 1) * HQ], v[1, q * HQ:(q + 1) * HQ]]
        quads.append(jnp.concatenate(fb + fb + [pad]))
    return jnp.stack(quads)[:, None, :]


def _dt_weight(w_dt):
    parts = []
    for q in range(NQ):
        fb = [w_dt[..., q * HQ:(q + 1) * HQ], w_dt[..., H_SSD + q * HQ:H_SSD + (q + 1) * HQ]]
        parts += fb + fb
    parts.append(jnp.zeros(w_dt.shape[:-1] + (DT_LANES - NQ * QUAD_LANES,), w_dt.dtype))
    return jnp.concatenate(parts, axis=-1)


def kernel(x_prompt, x_sample, state_ret, state_ssd, c, c_ctx, w_ada, b_ada, norm1_w, w_in,
           ret_log_decay, conv_w, conv_b, dt_bias, a_log, d_skip, ssd_norm_w, w_out, norm2_w,
           w_ff1, w_ff2, final_norm_w):
    bp, sp, _ = x_prompt.shape
    bs, ss, _ = x_sample.shape

    cond = jnp.zeros((COND_ROWS, D_MODEL), F32).at[:bs].set(c).at[bs].set(c_ctx)
    mod = _ada(cond, w_ada, b_ada).reshape(DEPTH, COND_ROWS, N_MOD, D_MODEL)

    rope = _rope_tables(ss)
    final_w = final_norm_w.reshape(1, D_MODEL)
    w_in_b = w_in.astype(BF16)
    w_dt_b = _dt_weight(w_in[:, :, D_MAIN:]).astype(BF16)
    w_out_b = w_out.astype(BF16)
    w_ff1_b = w_ff1.astype(BF16)
    w_ff2_b = w_ff2.astype(BF16)

    def run_group(x, mod_rows, rope_tabs, states, new_states):
        b, seq, _ = x.shape
        x2 = x.reshape(b * seq, D_MODEL)
        rows_per_cond = seq if mod_rows.stop - mod_rows.start > 1 else b * seq
        new_ret, new_ssd = new_states
        for l in range(DEPTH):
            mv = [mod[l, mod_rows, k][:, None, :] for k in range(N_MOD)]
            sh1, sc1, g1, sh2, sc2, g2 = mv
            proj, dt = _in_proj(x2, rows_per_cond, norm1_w[l].reshape(1, D_MODEL), sc1, sh1,
                                w_in_b, l, w_dt_b[l])
            init_r, init_s = (None, None) if states is None else states
            blocks, n_seq = (b, 1) if states is not None else (1, b)
            proj3 = proj.reshape(blocks, n_seq * seq, D_MAIN)
            dt3 = dt.reshape(blocks, n_seq * seq, NQ * DT_LANES)
            ret = _retention(proj3, ret_log_decay[l], rope_tabs, init_r, l, new_ret, n_seq=n_seq)
            ssd = _ssd(proj3, dt3, d_skip[l], _quad_lanes(dt_bias[l]), _quad_lanes(a_log[l]),
                       conv_w[l], conv_b[l].reshape(1, D_XBC), init_s, l, new_ssd, n_seq=n_seq)
            if new_ret is not None:
                new_ret, new_ssd = ret[1], ssd[1]
            x2 = _out_proj(ret[0].reshape(b * seq, D_RET), ssd[0].reshape(b * seq, D_SSD),
                           ssd_norm_w[l].reshape(1, D_SSD), x2, rows_per_cond, g1, w_out_b, l)
            x2 = _ffn(x2, rows_per_cond, norm2_w[l].reshape(1, D_MODEL), sc2, sh2, g2,
                      w_ff1_b, w_ff2_b, l, final_w if l == DEPTH - 1 else None)
        return x2.reshape(b, seq, D_MODEL), new_ret, new_ssd

    empty_states = (jnp.zeros((bp,) + state_ret.shape[1:], F32), jnp.zeros((bp,) + state_ssd.shape[1:], F32))
    y_prompt, new_state_ret, new_state_ssd = run_group(x_prompt, slice(bs, bs + 1), None, None, empty_states)
    y_sample, _, _ = run_group(x_sample, slice(0, bs), rope, (state_ret, state_ssd), (None, None))
    return (y_prompt, y_sample, new_state_ret, new_state_ssd)
```

```python
import functools

import jax
import jax.numpy as jnp
from jax import lax
from jax.experimental import pallas as pl
from jax.experimental.pallas import tpu as pltpu

F32 = jnp.float32
BF16 = jnp.bfloat16

D_MODEL = 2048
DEPTH = 4
CHUNK = 128
H_RET = 8
DK_RET = 128
DV_RET = 128
D_RET = H_RET * DV_RET
H_SSD = 16
P_SSD = 64
D_SSD = H_SSD * P_SSD
G_SSD = 2
N_SSD = 128
D_CONV = 3
D_XBC = D_SSD + 2 * G_SSD * N_SSD
D_FF = 4 * D_MODEL
N_MOD = 6
EPS = 1e-6
ROPE_BASE = 10000.0
GRID_W = 64

D_MAIN = 4 * D_RET + D_SSD + D_XBC
OFF_Q, OFF_K, OFF_V, OFF_G = 0, D_RET, 2 * D_RET, 3 * D_RET
OFF_Z = 4 * D_RET
OFF_XS = OFF_Z + D_SSD
OFF_B = OFF_XS + D_SSD
OFF_C = OFF_B + G_SSD * N_SSD

HQ = 4
NQ = H_SSD // HQ
HP = HQ // 2
DT_LANES = 128
QUAD_LANES = 4 * HQ
COND_ROWS = 16
SCAN_UNROLL = 32
SSD_UNROLL = 8
V7X_VMEM_LIMIT = 56 * 1024 * 1024


def _silu(x):
    return x * jax.nn.sigmoid(x)


def _softplus(x):
    return jnp.maximum(x, 0.0) + jnp.log1p(jnp.exp(-jnp.abs(x)))


def _rms(x, w):
    return x * lax.rsqrt(jnp.mean(x * x, axis=-1, keepdims=True) + EPS) * w


def _params(sem):
    return pltpu.CompilerParams(dimension_semantics=sem, vmem_limit_bytes=V7X_VMEM_LIMIT)


def _ada_kernel(cond_ref, w_ref, b_ref, o_ref):
    s = _silu(cond_ref[...]).astype(BF16)
    o_ref[...] = jnp.dot(s, w_ref[...].astype(BF16), preferred_element_type=F32) + b_ref[...]


def _ada(cond, w_ada, b_ada):
    tn = 2048
    n = N_MOD * D_MODEL
    return pl.pallas_call(
        _ada_kernel,
        grid=(DEPTH, n // tn),
        in_specs=[
            pl.BlockSpec((COND_ROWS, D_MODEL), lambda l, j: (0, 0)),
            pl.BlockSpec((None, D_MODEL, tn), lambda l, j: (l, 0, j)),
            pl.BlockSpec((None, 1, tn), lambda l, j: (l, 0, j)),
        ],
        out_specs=pl.BlockSpec((None, COND_ROWS, tn), lambda l, j: (l, 0, j)),
        out_shape=jax.ShapeDtypeStruct((DEPTH, COND_ROWS, n), F32),
        compiler_params=_params(("arbitrary", "arbitrary")),
        name="ada_mod",
    )(cond, w_ada, b_ada.reshape(DEPTH, 1, n))


def _modulated_norm(x, nw, sc, sh):
    return (_rms(x, nw) * (1.0 + sc) + sh).astype(BF16)


def _in_kernel(x_ref, nw_ref, sc_ref, sh_ref, w_ref, wdt_ref, o_ref, dt_ref, h_cur, h_next, *, nj):
    i, j = pl.program_id(0), pl.program_id(1)
    part = x_ref.shape[0] // nj
    rows = pl.ds(pl.multiple_of(j * part, part), part)

    def stage():
        h_next[rows, :] = _modulated_norm(x_ref[rows, :], nw_ref[...], sc_ref[...], sh_ref[...])

    @pl.when(i == 0)
    def _():
        stage()

    @pl.when(i > 0)
    def _():
        stage()
        dt = jnp.dot(h_cur[rows, :], wdt_ref[...], preferred_element_type=F32)
        for q in range(NQ):
            shift = (DT_LANES - QUAD_LANES * q) % DT_LANES
            dt_ref[rows, q * DT_LANES:(q + 1) * DT_LANES] = pltpu.roll(dt, shift, 1) if shift else dt
        o_ref[...] = jnp.dot(h_cur[...], w_ref[...], preferred_element_type=F32).astype(o_ref.dtype)

    @pl.when(j == nj - 1)
    def _():
        h_cur[...] = h_next[...]


def _in_proj(x2, rows_per_cond, nw, sc, sh, w_in_all, layer, w_dt):
    m = x2.shape[0]
    tm, tn = 512, D_MAIN // 2
    nt = m // tm
    staged = lambda i: jnp.minimum(i, nt - 1)
    done = lambda i: jnp.maximum(i - 1, 0)
    cond_map = lambda i, j: ((staged(i) * tm) // rows_per_cond, 0, 0)
    return pl.pallas_call(
        functools.partial(_in_kernel, nj=D_MAIN // tn),
        grid=(nt + 1, D_MAIN // tn),
        in_specs=[
            pl.BlockSpec((tm, D_MODEL), lambda i, j: (staged(i), 0)),
            pl.BlockSpec((1, D_MODEL), lambda i, j: (0, 0)),
            pl.BlockSpec((None, 1, D_MODEL), cond_map),
            pl.BlockSpec((None, 1, D_MODEL), cond_map),
            pl.BlockSpec((None, D_MODEL, tn), lambda i, j: (layer, 0, jnp.where(i > 0, j, 0))),
            pl.BlockSpec((D_MODEL, DT_LANES), lambda i, j: (0, 0)),
        ],
        out_specs=[
            pl.BlockSpec((tm, tn), lambda i, j: (done(i), jnp.where(i > 0, j, 0))),
            pl.BlockSpec((tm, NQ * DT_LANES), lambda i, j: (done(i), 0)),
        ],
        out_shape=[
            jax.ShapeDtypeStruct((m, D_MAIN), BF16),
            jax.ShapeDtypeStruct((m, NQ * DT_LANES), F32),
        ],
        scratch_shapes=[pltpu.VMEM((tm, D_MODEL), BF16), pltpu.VMEM((tm, D_MODEL), BF16)],
        compiler_params=_params(("arbitrary", "arbitrary")),
        name="in_proj",
    )(x2, nw, sc, sh, w_in_all, w_dt)


def _ret_kernel(*refs, seq, n_seq, rope, has_init, want_state):
    it = iter(refs)
    ld_ref, q_ref, k_ref, v_ref, g_ref = (next(it) for _ in range(5))
    cos_ref = sin_ref = init_ref = st_ref = None
    if rope:
        cos_ref, sin_ref = next(it), next(it)
    if has_init:
        init_ref = next(it)
    if want_state:
        next(it)
    y_ref = next(it)
    if want_state:
        st_ref = next(it)
    q_scr, kt_scr, sf_scr, sb_scr = next(it), next(it), next(it), next(it)

    t = CHUNK
    nc = seq // t
    head = pl.program_id(1)
    la_f = ld_ref[0, head]
    la_b = ld_ref[1, head]

    ii = lax.broadcasted_iota(jnp.int32, (t, t), 0)
    jj = lax.broadcasted_iota(jnp.int32, (t, t), 1)
    dist = (ii - jj).astype(F32)
    lower = ii >= jj
    upper = ii <= jj
    dmat = (jnp.where(lower, jnp.exp(la_f * jnp.where(lower, dist, 0.0)), 0.0)
            + jnp.where(upper, jnp.exp(la_b * jnp.where(upper, -dist, 0.0)), 0.0))
    pos = lax.broadcasted_iota(jnp.int32, (t, 1), 0).astype(F32)
    lane = lax.broadcasted_iota(jnp.int32, (1, t), 1).astype(F32)
    e_f = jnp.exp(la_f * (pos + 1.0))
    e_b = jnp.exp(la_b * (t - pos))
    w_f = jnp.exp(la_f * (t - 1.0 - lane))
    w_b = jnp.exp(la_b * lane)
    dec_f = jnp.exp(jnp.full((1, 1), t, F32) * la_f)
    dec_b = jnp.exp(jnp.full((1, 1), t, F32) * la_b)
    scale = DK_RET ** -0.5
    unroll = min(nc, SCAN_UNROLL)

    def rows(c):
        return pl.ds(pl.multiple_of(c * t, t), t)

    def prep(c, carry):
        r = rows(c)
        q = q_ref[r, :].astype(F32)
        k = k_ref[r, :].astype(F32)
        if rope:
            cs, sn = cos_ref[r, :], sin_ref[r, :]
            q = q * cs + pltpu.roll(q, DK_RET // 2, 1) * sn
            k = k * cs + pltpu.roll(k, DK_RET // 2, 1) * sn
        q_scr[r, :] = (q * scale).astype(BF16)
        kt_scr[c] = k.T.astype(BF16)
        return carry

    lax.fori_loop(0, nc, prep, 0, unroll=unroll)

    def chunk_state(c, w_row):
        kw = (kt_scr[c].astype(F32) * w_row).astype(BF16)
        return jnp.dot(kw, v_ref[rows(c), :], preferred_element_type=F32)

    def states(i, carry, base=0, last=nc - 1):
        s_f, s_b = carry
        c_f, c_b = base + i, base + last - i
        sf_scr[c_f] = s_f.astype(BF16)
        sb_scr[c_b] = s_b.astype(BF16)
        return dec_f * s_f + chunk_state(c_f, w_f), dec_b * s_b + chunk_state(c_b, w_b)

    def outs(c, carry):
        r = rows(c)
        q = q_scr[r, :]
        att = jnp.dot(q, kt_scr[c], preferred_element_type=F32)
        qf = q.astype(F32)
        a = jnp.concatenate([(att * dmat).astype(BF16), (qf * e_f).astype(BF16),
                             (qf * e_b).astype(BF16)], axis=1)
        rhs = jnp.concatenate([v_ref[r, :], sf_scr[c], sb_scr[c]], axis=0)
        y = jnp.dot(a, rhs, preferred_element_type=F32)
        yn = y * lax.rsqrt(jnp.mean(y * y, axis=-1, keepdims=True) + EPS)
        y_ref[r, :] = (_silu(g_ref[r, :].astype(F32)) * yn).astype(y_ref.dtype)
        return carry

    zero = jnp.zeros((DK_RET, DV_RET), F32)
    if n_seq == 1:
        s0_f, s0_b = (init_ref[0].T, init_ref[1].T) if has_init else (zero, zero)
        s_f, s_b = lax.fori_loop(0, nc, states, (s0_f, s0_b), unroll=unroll)
        if want_state:
            st_ref[0, 0] = s_f.T
            st_ref[0, 1] = s_b.T
    else:
        assert not has_init and not rope
        per = nc // n_seq

        def sequence(g, carry):
            s_f = s_b = zero
            for u in range(per):
                s_f, s_b = states(u, (s_f, s_b), base=g * per, last=per - 1)
            if want_state:
                st_ref[g, 0] = s_f.T
                st_ref[g, 1] = s_b.T
            return carry

        lax.fori_loop(0, n_seq, sequence, 0, unroll=min(n_seq, max(1, SCAN_UNROLL // per)))
    lax.fori_loop(0, nc, outs, 0, unroll=unroll)


def _retention(proj3, ret_ld, rope_tabs, init, layer, state_buf, n_seq=1):
    b, seq, _ = proj3.shape
    want_state = state_buf is not None
    aliases = {}
    hblk = lambda off: pl.BlockSpec((None, seq, DK_RET), lambda i, h, o=off // DK_RET: (i, 0, o + h))
    in_specs = [pl.BlockSpec(memory_space=pltpu.SMEM), hblk(OFF_Q), hblk(OFF_K), hblk(OFF_V), hblk(OFF_G)]
    args = [ret_ld, proj3, proj3, proj3, proj3]
    if rope_tabs is not None:
        in_specs += [pl.BlockSpec((seq, DK_RET), lambda i, h: (0, 0))] * 2
        args += list(rope_tabs)
    if init is not None:
        in_specs.append(pl.BlockSpec((None, None, 2, None, DV_RET, DK_RET),
                                     lambda i, h: (i, layer, 0, h, 0, 0)))
        args.append(init)
    out_specs = [pl.BlockSpec((None, seq, DV_RET), lambda i, h: (i, 0, h))]
    out_shape = [jax.ShapeDtypeStruct((b, seq, D_RET), BF16)]
    if want_state:
        aliases[len(args)] = 1
        in_specs.append(pl.BlockSpec(memory_space=pl.ANY))
        args.append(state_buf)
        out_specs.append(pl.BlockSpec((n_seq, None, 2, None, DV_RET, DK_RET),
                                      lambda i, h: (i, layer, 0, h, 0, 0)))
        out_shape.append(jax.ShapeDtypeStruct(state_buf.shape, state_buf.dtype))
    return pl.pallas_call(
        functools.partial(_ret_kernel, seq=seq, n_seq=n_seq, rope=rope_tabs is not None,
                          has_init=init is not None, want_state=want_state),
        grid=(b, H_RET),
        in_specs=in_specs,
        out_specs=out_specs,
        out_shape=out_shape,
        scratch_shapes=[pltpu.VMEM((seq, DK_RET), BF16),
                        pltpu.VMEM((seq // CHUNK, DK_RET, CHUNK), BF16),
                        pltpu.VMEM((seq // CHUNK, DK_RET, DV_RET), BF16),
                        pltpu.VMEM((seq // CHUNK, DK_RET, DV_RET), BF16)],
        input_output_aliases=aliases,
        compiler_params=_params(("arbitrary", "arbitrary")),
        name="retention",
    )(*args)


def _conv_silu_chunk(src_ref, w_ref, b_ref, c, nc, per):
    t = CHUNK
    r0 = pl.multiple_of(c * t, t)
    cur = src_ref[pl.ds(r0, t), :].astype(F32)
    p0 = pl.multiple_of(jnp.maximum(r0 - 16, 0), 16)
    n0 = pl.multiple_of(jnp.minimum(r0 + t, nc * t - 16), 16)
    pos = lax.rem(c, per)
    prev = jnp.where(pos > 0, src_ref[pl.ds(p0, 16), :].astype(F32)[15:16, :], 0.0)
    nxt = jnp.where(pos < per - 1, src_ref[pl.ds(n0, 16), :].astype(F32)[0:1, :], 0.0)
    row = lax.broadcasted_iota(jnp.int32, (t, 1), 0)
    xm1 = jnp.where(row == 0, prev, pltpu.roll(cur, 1, 0))
    xp1 = jnp.where(row == t - 1, nxt, pltpu.roll(cur, t - 1, 0))
    return _silu(xm1 * w_ref[0:1, :] + cur * w_ref[1:2, :] + xp1 * w_ref[2:3, :] + b_ref[...])


def _exact_cumsum(tri_b, x):
    hi = x.astype(BF16)
    r1 = x - hi.astype(F32)
    mid = r1.astype(BF16)
    lo = (r1 - mid.astype(F32)).astype(BF16)
    return (jnp.dot(tri_b, hi, preferred_element_type=F32)
            + jnp.dot(tri_b, mid, preferred_element_type=F32)
            + jnp.dot(tri_b, lo, preferred_element_type=F32))


def _ssd_kernel(*refs, seq, n_seq, has_init, want_state):
    it = iter(refs)
    (dsk_ref, xs_ref, z_ref, b_ref, c_ref, dt_ref, bias_ref, alog_ref,
     wx_ref, bx_ref, wb_ref, bb_ref, wc_ref, bc_ref) = (next(it) for _ in range(14))
    init_ref = st_ref = None
    if has_init:
        init_ref = next(it)
    if want_state:
        next(it)
    u_ref = next(it)
    if want_state:
        st_ref = next(it)
    xc_scr, bt_scr, cc_scr, col_scr, row_scr, sf_scr, sb_scr = (next(it) for _ in range(7))

    t = CHUNK
    nc = seq // t
    per = nc // n_seq
    quad = pl.program_id(1)
    unroll = min(nc, SSD_UNROLL)

    ii = lax.broadcasted_iota(jnp.int32, (t, t), 0)
    jj = lax.broadcasted_iota(jnp.int32, (t, t), 1)
    lower = ii >= jj
    strict_lower = ii > jj
    diag = ii == jj
    tril_b = jnp.where(lower, 1.0, 0.0).astype(BF16)
    lane = lax.broadcasted_iota(jnp.int32, (1, DT_LANES), 1)
    a_neg = -jnp.exp(alog_ref[...])
    dt_bias = bias_ref[...]

    def rows(c):
        return pl.ds(pl.multiple_of(c * t, t), t)

    def prep_group(c, carry):
        bt_scr[c] = _conv_silu_chunk(b_ref, wb_ref, bb_ref, c, nc, per).T.astype(BF16)
        cc_scr[rows(c), :] = _conv_silu_chunk(c_ref, wc_ref, bc_ref, c, nc, per).astype(BF16)
        return carry

    @pl.when(quad % (NQ // G_SSD) == 0)
    def _():
        lax.fori_loop(0, nc, prep_group, 0, unroll=unroll)

    def prep(c, carry):
        r = rows(c)
        xc_scr[r, :] = _conv_silu_chunk(xs_ref, wx_ref, bx_ref, c, nc, per).astype(BF16)
        dt = _softplus(dt_ref[r, :] + dt_bias)
        da = dt * a_neg
        cum = _exact_cumsum(tril_b, da)
        total = cum[t - 1:t, :]
        decay = jnp.where(lane < HQ, cum, total - cum + da)
        packed = jnp.where(lane < 2 * HQ, decay, dt)
        col_scr[r, :] = packed
        row_scr[c] = packed.T[0:4 * HQ, :]
        return carry

    lax.fori_loop(0, nc, prep, 0, unroll=min(nc, 2 * SSD_UNROLL))

    even = lax.broadcasted_iota(jnp.int32, (1, 2 * P_SSD), 1) < P_SSD

    def pair_cols(k):
        return slice(k * 2 * P_SSD, (k + 1) * 2 * P_SSD)

    def chunk_states(c, first, tot_lane):
        rowf = row_scr[c]
        cum_r = rowf[first:first + HQ, :]
        dt_r = rowf[2 * HQ + first:3 * HQ + first, :]
        tot = cum_r[:, tot_lane:tot_lane + 1]
        w = jnp.exp(tot - cum_r) * dt_r
        dec = jnp.exp(tot)
        btf = bt_scr[c].astype(F32)
        r = rows(c)
        decs, css = [], []
        for k in range(HP):
            x_pair = xc_scr[r, pair_cols(k)]
            cs = [jnp.dot((btf * w[hh:hh + 1, :]).astype(BF16), x_pair, preferred_element_type=F32)
                  for hh in (2 * k, 2 * k + 1)]
            css.append(jnp.where(even, cs[0], cs[1]))
            decs.append(jnp.where(even, dec[2 * k:2 * k + 1, :], dec[2 * k + 1:2 * k + 2, :]))
        return decs, css

    def states(i, carry, base=0, last=nc - 1):
        s_f, s_b = carry
        c_f, c_b = base + i, base + last - i
        dec_f, cs_f = chunk_states(c_f, 0, t - 1)
        dec_b, cs_b = chunk_states(c_b, HQ, 0)
        new_f, new_b = [], []
        for k in range(HP):
            sf_scr[c_f * HP + k] = s_f[k].astype(BF16)
            sb_scr[c_b * HP + k] = s_b[k].astype(BF16)
            new_f.append(dec_f[k] * s_f[k] + cs_f[k])
            new_b.append(dec_b[k] * s_b[k] + cs_b[k])
        return tuple(new_f), tuple(new_b)

    def head_lhs(colf, rowf, cb, cf, hh):
        fl, bl = hh, HQ + hh
        rc_f = jnp.broadcast_to(colf[:, fl:fl + 1], (t, t))
        rc_b = jnp.broadcast_to(colf[:, bl:bl + 1], (t, t))
        seg = jnp.where(lower, rc_f - rowf[fl:fl + 1, :], rc_b - rowf[bl:bl + 1, :])
        dt_f = rowf[2 * HQ + fl:2 * HQ + fl + 1, :]
        dt_b = rowf[2 * HQ + bl:2 * HQ + bl + 1, :]
        dt_sel = jnp.where(strict_lower, dt_f, jnp.where(diag, dt_f + dt_b, dt_b))
        return jnp.concatenate([(cb * (jnp.exp(seg) * dt_sel)).astype(BF16),
                                (cf * jnp.exp(rc_f)).astype(BF16),
                                (cf * jnp.exp(rc_b)).astype(BF16)], axis=1)

    def outs(c, carry):
        r = rows(c)
        colf = col_scr[r, :]
        rowf = row_scr[c]
        cmat = cc_scr[r, :]
        cf = cmat.astype(F32)
        cb = jnp.dot(cmat, bt_scr[c], preferred_element_type=F32)
        for k in range(HP):
            x_pair = xc_scr[r, pair_cols(k)]
            rhs = jnp.concatenate([x_pair, sf_scr[c * HP + k], sb_scr[c * HP + k]], axis=0)
            ys = [jnp.dot(head_lhs(colf, rowf, cb, cf, hh), rhs, preferred_element_type=F32)
                  for hh in (2 * k, 2 * k + 1)]
            h0 = quad * HQ + 2 * k
            skip = jnp.where(even, dsk_ref[h0], dsk_ref[h0 + 1])
            y = jnp.where(even, ys[0], ys[1]) + skip * x_pair.astype(F32)
            u_ref[r, pair_cols(k)] = (y * _silu(z_ref[r, pair_cols(k)].astype(F32))).astype(u_ref.dtype)
        return carry

    def pair_state(ref, d, k):
        return ref[d, 2 * k:2 * k + 2].reshape(2 * P_SSD, N_SSD).T

    def put_states(g, s_f, s_b):
        for k in range(HP):
            st_ref[g, 0, 2 * k:2 * k + 2] = s_f[k].T.reshape(2, P_SSD, N_SSD)
            st_ref[g, 1, 2 * k:2 * k + 2] = s_b[k].T.reshape(2, P_SSD, N_SSD)

    zero = tuple(jnp.zeros((N_SSD, 2 * P_SSD), F32) for _ in range(HP))
    if n_seq == 1:
        s0_f, s0_b = zero, zero
        if has_init:
            s0_f = tuple(pair_state(init_ref, 0, k) for k in range(HP))
            s0_b = tuple(pair_state(init_ref, 1, k) for k in range(HP))
        s_f, s_b = lax.fori_loop(0, nc, states, (s0_f, s0_b), unroll=min(nc, 2 * SSD_UNROLL))
        if want_state:
            put_states(0, s_f, s_b)
    else:
        assert not has_init

        def sequence(g, carry):
            s_f = s_b = zero
            for u in range(per):
                s_f, s_b = states(u, (s_f, s_b), base=g * per, last=per - 1)
            if want_state:
                put_states(g, s_f, s_b)
            return carry

        lax.fori_loop(0, n_seq, sequence, 0, unroll=min(n_seq, max(1, 2 * SSD_UNROLL // per)))
    lax.fori_loop(0, nc, outs, 0, unroll=min(nc, 2 * SSD_UNROLL))


def _ssd(proj3, dt3, d_skip, dt_bias_q, a_log_q, conv_w, conv_b, init, layer, state_buf, n_seq=1):
    b, seq, _ = proj3.shape
    want_state = state_buf is not None
    aliases = {}
    wq = HQ * P_SSD
    col = lambda width, off, fn: pl.BlockSpec((None, seq, width),
                                              lambda i, q, o=off // width: (i, 0, o + fn(q)))
    ident = lambda q: q
    group = lambda q: q // (NQ // G_SSD)
    xoff = 0
    boff = D_SSD
    coff = D_SSD + G_SSD * N_SSD
    cw = lambda width, off, fn: pl.BlockSpec((D_CONV, width), lambda i, q, o=off // width: (0, o + fn(q)))
    cbias = lambda width, off, fn: pl.BlockSpec((1, width), lambda i, q, o=off // width: (0, o + fn(q)))
    in_specs = [
        pl.BlockSpec(memory_space=pltpu.SMEM),
        col(wq, OFF_XS, ident), col(wq, OFF_Z, ident), col(N_SSD, OFF_B, group), col(N_SSD, OFF_C, group),
        pl.BlockSpec((None, seq, DT_LANES), lambda i, q: (i, 0, q)),
        pl.BlockSpec((None, 1, DT_LANES), lambda i, q: (q, 0, 0)),
        pl.BlockSpec((None, 1, DT_LANES), lambda i, q: (q, 0, 0)),
        cw(wq, xoff, ident), cbias(wq, xoff, ident),
        cw(N_SSD, boff, group), cbias(N_SSD, boff, group),
        cw(N_SSD, coff, group), cbias(N_SSD, coff, group),
    ]
    args = [d_skip, proj3, proj3, proj3, proj3, dt3, dt_bias_q, a_log_q,
            conv_w, conv_b, conv_w, conv_b, conv_w, conv_b]
    st_spec = pl.BlockSpec((None, None, 2, HQ, P_SSD, N_SSD), lambda i, q: (i, layer, 0, q, 0, 0))
    if init is not None:
        in_specs.append(st_spec)
        args.append(init)
    out_specs = [pl.BlockSpec((None, seq, wq), lambda i, q: (i, 0, q))]
    out_shape = [jax.ShapeDtypeStruct((b, seq, D_SSD), BF16)]
    if want_state:
        aliases[len(args)] = 1
        in_specs.append(pl.BlockSpec(memory_space=pl.ANY))
        args.append(state_buf)
        out_specs.append(pl.BlockSpec((n_seq, None, 2, HQ, P_SSD, N_SSD),
                                      lambda i, q: (i, layer, 0, q, 0, 0)))
        out_shape.append(jax.ShapeDtypeStruct(state_buf.shape, state_buf.dtype))
    nc = seq // CHUNK
    return pl.pallas_call(
        functools.partial(_ssd_kernel, seq=seq, n_seq=n_seq, has_init=init is not None,
                          want_state=want_state),
        grid=(b, NQ),
        in_specs=in_specs,
        out_specs=out_specs,
        out_shape=out_shape,
        scratch_shapes=[
            pltpu.VMEM((seq, wq), BF16), pltpu.VMEM((nc, N_SSD, CHUNK), BF16), pltpu.VMEM((seq, N_SSD), BF16),
            pltpu.VMEM((seq, DT_LANES), F32), pltpu.VMEM((nc, 4 * HQ, CHUNK), F32),
            pltpu.VMEM((nc * HP, N_SSD, 2 * P_SSD), BF16), pltpu.VMEM((nc * HP, N_SSD, 2 * P_SSD), BF16),
        ],
        input_output_aliases=aliases,
        compiler_params=_params(("arbitrary", "arbitrary")),
        name="ssd",
    )(*args)


def _out_kernel(yr_ref, us_ref, nw_ref, x_ref, g_ref, w_ref, o_ref):
    a = jnp.concatenate([yr_ref[...], _rms(us_ref[...].astype(F32), nw_ref[...]).astype(BF16)], axis=1)
    o_ref[...] = x_ref[...] + g_ref[...] * jnp.dot(a, w_ref[...], preferred_element_type=F32)


def _out_proj(yr, us, ssd_nw, x2, rows_per_cond, gate, w_out_all, layer):
    m = x2.shape[0]
    tm = 512
    return pl.pallas_call(
        _out_kernel,
        grid=(m // tm,),
        in_specs=[
            pl.BlockSpec((tm, D_RET), lambda i: (i, 0)),
            pl.BlockSpec((tm, D_SSD), lambda i: (i, 0)),
            pl.BlockSpec((1, D_SSD), lambda i: (0, 0)),
            pl.BlockSpec((tm, D_MODEL), lambda i: (i, 0)),
            pl.BlockSpec((None, 1, D_MODEL), lambda i: ((i * tm) // rows_per_cond, 0, 0)),
            pl.BlockSpec((None, D_RET + D_SSD, D_MODEL), lambda i: (layer, 0, 0)),
        ],
        out_specs=pl.BlockSpec((tm, D_MODEL), lambda i: (i, 0)),
        out_shape=jax.ShapeDtypeStruct((m, D_MODEL), F32),
        compiler_params=_params(("arbitrary",)),
        name="out_proj",
    )(yr, us, ssd_nw, x2, gate, w_out_all)


def _ffn_kernel(*refs, final, nf):
    it = iter(refs)
    x_ref, res_ref, nw_ref, sc_ref, sh_ref, g_ref, w1_ref, w2_ref = (next(it) for _ in range(8))
    fw_ref = next(it) if final else None
    o_ref, h_cur, h_next, acc_scr = next(it), next(it), next(it), next(it)
    i, f = pl.program_id(0), pl.program_id(1)
    part = x_ref.shape[0] // nf
    rows = pl.ds(pl.multiple_of(f * part, part), part)

    def stage():
        h_next[rows, :] = _modulated_norm(x_ref[rows, :], nw_ref[...], sc_ref[...], sh_ref[...])

    def step(first):
        stage()
        a = jnp.maximum(jnp.dot(h_cur[...], w1_ref[...], preferred_element_type=F32), 0.0)
        d = jnp.dot((a * a).astype(BF16), w2_ref[...], preferred_element_type=F32)
        if first:
            acc_scr[...] = d
        else:
            acc_scr[...] += d

    pl.when(i == 0)(stage)
    pl.when((i > 0) & (f == 0))(functools.partial(step, True))
    pl.when((i > 0) & (f > 0))(functools.partial(step, False))

    @pl.when((i > 0) & (f == nf - 1))
    def _():
        y = res_ref[...] + g_ref[...] * acc_scr[...]
        if final:
            y = _rms(y, fw_ref[...])
        o_ref[...] = y

    @pl.when(f == nf - 1)
    def _():
        h_cur[...] = h_next[...]


def _ffn(x2, rows_per_cond, nw, sc, sh, gate, w1, w2, layer, final_w):
    m = x2.shape[0]
    tm, tf = 512, 1024
    nt, nf = m // tm, D_FF // tf
    staged = lambda i: jnp.minimum(i, nt - 1)
    done = lambda i: jnp.maximum(i - 1, 0)
    wtile = lambda i, f: jnp.where(i > 0, f, 0)
    vec = pl.BlockSpec((1, D_MODEL), lambda i, f: (0, 0))
    cvec = lambda tile: pl.BlockSpec((None, 1, D_MODEL),
                                     lambda i, f: ((tile(i) * tm) // rows_per_cond, 0, 0))
    in_specs = [pl.BlockSpec((tm, D_MODEL), lambda i, f: (staged(i), 0)),
                pl.BlockSpec((tm, D_MODEL), lambda i, f: (done(i), 0)),
                vec, cvec(staged), cvec(staged), cvec(done),
                pl.BlockSpec((None, D_MODEL, tf), lambda i, f: (layer, 0, wtile(i, f))),
                pl.BlockSpec((None, tf, D_MODEL), lambda i, f: (layer, wtile(i, f), 0))]
    args = [x2, x2, nw, sc, sh, gate, w1, w2]
    if final_w is not None:
        in_specs.append(vec)
        args.append(final_w)
    return pl.pallas_call(
        functools.partial(_ffn_kernel, final=final_w is not None, nf=nf),
        grid=(nt + 1, nf),
        in_specs=in_specs,
        out_specs=pl.BlockSpec((tm, D_MODEL), lambda i, f: (done(i), 0)),
        out_shape=jax.ShapeDtypeStruct((m, D_MODEL), F32),
        scratch_shapes=[pltpu.VMEM((tm, D_MODEL), BF16), pltpu.VMEM((tm, D_MODEL), BF16),
                        pltpu.VMEM((tm, D_MODEL), F32)],
        compiler_params=_params(("arbitrary", "arbitrary")),
        name="ffn",
    )(*args)


def _rope_tables(seq):
    pos = jnp.arange(seq)
    row = (pos // GRID_W).astype(F32)
    col = (pos % GRID_W).astype(F32)
    half = DK_RET // 2
    inv = 1.0 / (ROPE_BASE ** (jnp.arange(0, half, 2, dtype=F32) / half))
    ang = jnp.concatenate([row[:, None] * inv, col[:, None] * inv], -1)
    cs, sn = jnp.cos(ang), jnp.sin(ang)
    return jnp.concatenate([cs, cs], -1), jnp.concatenate([-sn, sn], -1)


def _quad_lanes(v):
    pad = jnp.zeros((DT_LANES - 4 * HQ,), v.dtype)
    quads = []
    for q in range(NQ):
        fb = [v[0, q * HQ:(q + 1) * HQ], v[1, q * HQ:(q + 1) * HQ]]
        quads.append(jnp.concatenate(fb + fb + [pad]))
    return jnp.stack(quads)[:, None, :]


def _dt_weight(w_dt):
    parts = []
    for q in range(NQ):
        fb = [w_dt[..., q * HQ:(q + 1) * HQ], w_dt[..., H_SSD + q * HQ:H_SSD + (q + 1) * HQ]]
        parts += fb + fb
    parts.append(jnp.zeros(w_dt.shape[:-1] + (DT_LANES - NQ * QUAD_LANES,), w_dt.dtype))
    return jnp.concatenate(parts, axis=-1)


def kernel(x_prompt, x_sample, state_ret, state_ssd, c, c_ctx, w_ada, b_ada, norm1_w, w_in,
           ret_log_decay, conv_w, conv_b, dt_bias, a_log, d_skip, ssd_norm_w, w_out, norm2_w,
           w_ff1, w_ff2, final_norm_w):
    bp, sp, _ = x_prompt.shape
    bs, ss, _ = x_sample.shape

    cond = jnp.zeros((COND_ROWS, D_MODEL), F32).at[:bs].set(c).at[bs].set(c_ctx)
    mod = _ada(cond, w_ada, b_ada).reshape(DEPTH, COND_ROWS, N_MOD, D_MODEL)

    rope = _rope_tables(ss)
    final_w = final_norm_w.reshape(1, D_MODEL)
    w_in_b = w_in.astype(BF16)
    w_dt_b = _dt_weight(w_in[:, :, D_MAIN:]).astype(BF16)
    w_out_b = w_out.astype(BF16)
    w_ff1_b = w_ff1.astype(BF16)
    w_ff2_b = w_ff2.astype(BF16)

    def run_group(x, mod_rows, rope_tabs, states, new_states):
        b, seq, _ = x.shape
        x2 = x.reshape(b * seq, D_MODEL)
        rows_per_cond = seq if mod_rows.stop - mod_rows.start > 1 else b * seq
        new_ret, new_ssd = new_states
        for l in range(DEPTH):
            mv = [mod[l, mod_rows, k][:, None, :] for k in range(N_MOD)]
            sh1, sc1, g1, sh2, sc2, g2 = mv
            proj, dt = _in_proj(x2, rows_per_cond, norm1_w[l].reshape(1, D_MODEL), sc1, sh1,
                                w_in_b, l, w_dt_b[l])
            init_r, init_s = (None, None) if states is None else states
            blocks, n_seq = (b, 1) if states is not None else (1, b)
            proj3 = proj.reshape(blocks, n_seq * seq, D_MAIN)
            dt3 = dt.reshape(blocks, n_seq * seq, NQ * DT_LANES)
            ret = _retention(proj3, ret_log_decay[l], rope_tabs, init_r, l, new_ret, n_seq=n_seq)
            ssd = _ssd(proj3, dt3, d_skip[l], _quad_lanes(dt_bias[l]), _quad_lanes(a_log[l]),
                       conv_w[l], conv_b[l].reshape(1, D_XBC), init_s, l, new_ssd, n_seq=n_seq)
            if new_ret is not None:
                new_ret, new_ssd = ret[1], ssd[1]
            x2 = _out_proj(ret[0].reshape(b * seq, D_RET), ssd[0].reshape(b * seq, D_SSD),
                           ssd_norm_w[l].reshape(1, D_SSD), x2, rows_per_cond, g1, w_out_b, l)
            x2 = _ffn(x2, rows_per_cond, norm2_w[l].reshape(1, D_MODEL), sc2, sh2, g2,
                      w_ff1_b, w_ff2_b, l, final_w if l == DEPTH - 1 else None)
        return x2.reshape(b, seq, D_MODEL), new_ret, new_ssd

    empty_states = (jnp.zeros((bp,) + state_ret.shape[1:], F32), jnp.zeros((bp,) + state_ssd.shape[1:], F32))
    y_prompt, new_state_ret, new_state_ssd = run_group(x_prompt, slice(bs, bs + 1), None, None, empty_states)
    y_sample, _, _ = run_group(x_sample, slice(0, bs), rope, (state_ret, state_ssd), (None, None))
    return (y_prompt, y_sample, new_state_ret, new_state_ssd)
```
